```python
import jax, jax.numpy as jnp
from jax import lax
import numpy as np

D_MODEL = 1024
BATCH = 8
SEQ = 2048
DEPTH = 1

N_MEM = 256
D_MIX = D_MODEL
D_SGU = D_MIX // 2
D_ATT = D_MIX - D_SGU
SGU_GROUPS = 8
SGU_GROUP_DIM = D_SGU // SGU_GROUPS
CHUNK = 128
N_HEADS = 8
HEAD_DIM = D_ATT // N_HEADS
DILATED_CONFIGS = ((128, 1), (512, 4), (2048, 16))
ROPE_THETA = 500000.0
ROPE_DIM = HEAD_DIM // 4
X_HEADS = 4
X_HEAD_DIM = D_MODEL // X_HEADS
D_FF = 2816
D_IN = 2 * D_SGU + 3 * D_ATT
NORM_EPS = 1e-6
LN_EPS = 1e-5
MAX_POS_OFFSET = 4096

kernel_name = "hybrid_sgu_dilated_attn_macaron_layer"

F32 = jnp.float32


def rmsnorm(t, g):
    tf = t.astype(F32)
    y = tf * lax.rsqrt(jnp.mean(tf * tf, axis=-1, keepdims=True) + NORM_EPS)
    return (y * g.astype(F32)).astype(t.dtype)


def group_layernorm(t, g, b):
    G, C = t.shape[-2], t.shape[-1]
    tf = t.astype(F32)
    mu = jnp.mean(tf, axis=-1, keepdims=True)
    d = tf - mu
    y = d * lax.rsqrt(jnp.mean(d * d, axis=-1, keepdims=True) + LN_EPS)
    return (y * g.astype(F32).reshape(G, C) + b.astype(F32).reshape(G, C)).astype(t.dtype)


def swiglu(h, w_gate, w_up, w_down):
    return (jax.nn.silu(h @ w_gate) * (h @ w_up)) @ w_down


def partial_rotary(t, positions):
    half = ROPE_DIM // 2
    inv_freq = ROPE_THETA ** (-2.0 * jnp.arange(half, dtype=F32) / ROPE_DIM)
    ang = positions.astype(F32)[..., None] * inv_freq
    cos = jnp.cos(ang)[:, :, None, :]
    sin = jnp.sin(ang)[:, :, None, :]
    tf = t.astype(F32)
    x1 = tf[..., :half]
    x2 = tf[..., half:ROPE_DIM]
    out = jnp.concatenate([x1 * cos - x2 * sin, x2 * cos + x1 * sin, tf[..., ROPE_DIM:]], axis=-1)
    return out.astype(t.dtype)


def chunked_spatial_gating(z, ln_g, ln_b, w_s, b_s):
    b, s, _ = z.shape
    u, v = z[..., :D_SGU], z[..., D_SGU:]
    v = group_layernorm(v.reshape(b, s, SGU_GROUPS, SGU_GROUP_DIM), ln_g, ln_b)
    v = v.reshape(b, s // CHUNK, CHUNK, SGU_GROUPS, SGU_GROUP_DIM)
    causal = jnp.tril(jnp.ones((CHUNK, CHUNK), dtype=bool))
    ws = w_s * causal.astype(w_s.dtype)
    mixed = jnp.einsum('gij,bnjgc->bnigc', ws, v) + b_s.T[:, :, None]
    return u * mixed.reshape(b, s, D_SGU)


def dilated_window_attention(q, k, v, window, dilation):
    b, s, h, e = q.shape
    band = window // dilation
    L = s // dilation
    nb = -(-L // band)
    Lp = nb * band
    qs = q.reshape(b, L, dilation, h, e)
    ks = k.reshape(b, L, dilation, h, e)
    vs = v.reshape(b, L, dilation, h, e)
    qb = jnp.pad(qs, ((0, 0), (0, Lp - L), (0, 0), (0, 0), (0, 0))).reshape(b, nb, band, dilation, h, e)

    def kv_blocks(t):
        t = jnp.pad(t, ((0, 0), (band, Lp - L), (0, 0), (0, 0), (0, 0))).reshape(b, nb + 1, band, dilation, h, e)
        return jnp.concatenate([t[:, :-1], t[:, 1:]], axis=2)

    kb, vb = kv_blocks(ks), kv_blocks(vs)
    scores = jnp.einsum('bnqrhe,bnkrhe->bnrhqk', qb.astype(F32), kb.astype(F32)) * (e ** -0.5)
    qi = jnp.arange(band)[:, None]
    kj = jnp.arange(2 * band)[None, :]
    dist = band + qi - kj
    key_pos = (jnp.arange(nb)[:, None, None] - 1) * band + kj[None]
    mask = ((dist >= 0) & (dist <= band))[None] & (key_pos >= 0)
    scores = jnp.where(mask[None, :, None, None], scores, -jnp.inf)
    lse = jax.nn.logsumexp(scores, axis=-1)
    p = jnp.exp(scores - lse[..., None])
    o = jnp.einsum('bnrhqk,bnkrhe->bnqrhe', p.astype(v.dtype), vb)
    o = o.reshape(b, Lp, dilation, h, e)[:, :L].reshape(b, s, h, e)
    lse = jnp.transpose(lse, (0, 1, 4, 2, 3)).reshape(b, Lp, dilation, h)[:, :L].reshape(b, s, h)
    return o, lse


def dilated_mixture_attention(q, k, v):
    outs, lses = [], []
    for window, dilation in DILATED_CONFIGS:
        o, l = dilated_window_attention(q, k, v, window, dilation)
        outs.append(o)
        lses.append(l)
    wts = jax.nn.softmax(jnp.stack(lses, axis=0), axis=0)
    return jnp.einsum('gbsh,gbshe->bshe', wts.astype(q.dtype), jnp.stack(outs, axis=0))


def memory_cross_attention(h, m, wq, wk, wv, wo):
    b, s, _ = h.shape
    nm = m.shape[1]
    q = (h @ wq).reshape(b, s, X_HEADS, X_HEAD_DIM)
    k = (m @ wk).reshape(b, nm, X_HEADS, X_HEAD_DIM)
    v = (m @ wv).reshape(b, nm, X_HEADS, X_HEAD_DIM)
    scores = jnp.einsum('bshe,bmhe->bhsm', q.astype(F32), k.astype(F32)) * (X_HEAD_DIM ** -0.5)
    p = jax.nn.softmax(scores, axis=-1)
    o = jnp.einsum('bhsm,bmhe->bshe', p.astype(v.dtype), v)
    return o.reshape(b, s, D_MODEL) @ wo


def setup_inputs(seed: int = 0) -> dict:
    key = jax.random.key(seed)
    ks = jax.random.split(key, 32)

    def w(k, shape, fan_in):
        return jax.random.normal(k, shape, F32) * (fan_in ** -0.5)

    def gain(k, shape):
        return 1.0 + 0.02 * jax.random.normal(k, shape, F32)

    L = DEPTH
    x = jax.random.normal(ks[0], (BATCH, SEQ, D_MODEL), F32)
    mem = jax.random.normal(ks[1], (BATCH, N_MEM, D_MODEL), F32)
    start = jax.random.randint(ks[2], (BATCH, 1), 0, MAX_POS_OFFSET, dtype=jnp.int32)
    positions = (start + jnp.arange(SEQ, dtype=jnp.int32)[None, :]).astype(jnp.int32)
    return {
        "x": x,
        "mem": mem,
        "positions": positions,
        "ffn1_norm": gain(ks[3], (L, D_MODEL)),
        "ffn1_w_gate": w(ks[4], (L, D_MODEL, D_FF), D_MODEL),
        "ffn1_w_up": w(ks[5], (L, D_MODEL, D_FF), D_MODEL),
        "ffn1_w_down": w(ks[6], (L, D_FF, D_MODEL), D_FF),
        "mix_norm": gain(ks[7], (L, D_MODEL)),
        "w_in": w(ks[8], (L, D_MODEL, D_IN), D_MODEL),
        "sgu_ln_g": gain(ks[9], (L, D_SGU)),
        "sgu_ln_b": 0.02 * jax.random.normal(ks[10], (L, D_SGU), F32),
        "sgu_w_s": w(ks[11], (L, SGU_GROUPS, CHUNK, CHUNK), CHUNK),
        "sgu_b_s": 1.0 + 0.1 * jax.random.normal(ks[12], (L, SGU_GROUPS, CHUNK), F32),
        "out_norm_a": gain(ks[13], (L, D_SGU)),
        "out_norm_b": gain(ks[14], (L, D_ATT)),
        "w_out": w(ks[15], (L, D_MIX, D_MODEL), D_MIX),
        "cross_norm": gain(ks[16], (L, D_MODEL)),
        "mem_norm": gain(ks[17], (L, D_MODEL)),
        "cross_wq": w(ks[18], (L, D_MODEL, D_MODEL), D_MODEL),
        "cross_wk": w(ks[19], (L, D_MODEL, D_MODEL), D_MODEL),
        "cross_wv": w(ks[20], (L, D_MODEL, D_MODEL), D_MODEL),
        "cross_wo": w(ks[21], (L, D_MODEL, D_MODEL), D_MODEL),
        "ffn2_norm": gain(ks[22], (L, D_MODEL)),
        "ffn2_w_gate": w(ks[23], (L, D_MODEL, D_FF), D_MODEL),
        "ffn2_w_up": w(ks[24], (L, D_MODEL, D_FF), D_MODEL),
        "ffn2_w_down": w(ks[25], (L, D_FF, D_MODEL), D_FF),
        "final_norm": gain(ks[26], (D_MODEL,)),
    }


def reference(x, mem, positions, ffn1_norm, ffn1_w_gate, ffn1_w_up, ffn1_w_down,
              mix_norm, w_in, sgu_ln_g, sgu_ln_b, sgu_w_s, sgu_b_s,
              out_norm_a, out_norm_b, w_out, cross_norm, mem_norm,
              cross_wq, cross_wk, cross_wv, cross_wo,
              ffn2_norm, ffn2_w_gate, ffn2_w_up, ffn2_w_down, final_norm):
    b, s, _ = x.shape
    for l in range(DEPTH):
        h = rmsnorm(x, ffn1_norm[l])
        x = x + 0.5 * swiglu(h, ffn1_w_gate[l], ffn1_w_up[l], ffn1_w_down[l])

        h = rmsnorm(x, mix_norm[l])
        z = h @ w_in[l]
        z_sgu = jax.nn.gelu(z[..., :2 * D_SGU], approximate=False)
        o0 = 2 * D_SGU
        q = z[..., o0:o0 + D_ATT].reshape(b, s, N_HEADS, HEAD_DIM)
        k = z[..., o0 + D_ATT:o0 + 2 * D_ATT].reshape(b, s, N_HEADS, HEAD_DIM)
        v = z[..., o0 + 2 * D_ATT:o0 + 3 * D_ATT].reshape(b, s, N_HEADS, HEAD_DIM)
        q = partial_rotary(q, positions)
        k = partial_rotary(k, positions)

        y_a = chunked_spatial_gating(z_sgu, sgu_ln_g[l], sgu_ln_b[l], sgu_w_s[l], sgu_b_s[l])
        y_b = dilated_mixture_attention(q, k, v).reshape(b, s, D_ATT)
        y = jnp.concatenate([rmsnorm(y_a, out_norm_a[l]), rmsnorm(y_b, out_norm_b[l])], axis=-1)
        x = x + y @ w_out[l]

        h = rmsnorm(x, cross_norm[l])
        m = rmsnorm(mem, mem_norm[l])
        x = x + memory_cross_attention(h, m, cross_wq[l], cross_wk[l], cross_wv[l], cross_wo[l])

        h = rmsnorm(x, ffn2_norm[l])
        x = x + 0.5 * swiglu(h, ffn2_w_gate[l], ffn2_w_up[l], ffn2_w_down[l])
    return rmsnorm(x, final_norm)
```

```python
import functools

import jax
import jax.numpy as jnp
from jax import lax
from jax.experimental import pallas as pl
from jax.experimental.pallas import tpu as pltpu

F32 = jnp.float32
BF16 = jnp.bfloat16

D_MODEL = 1024
N_MEM = 256
D_SGU = 512
D_ATT = 512
SGU_GROUPS = 8
SGU_GROUP_DIM = D_SGU // SGU_GROUPS
CHUNK = 128
N_HEADS = 8
HEAD_DIM = D_ATT // N_HEADS
DILATIONS = (1, 4, 16)
BAND = 128
ROPE_THETA = 500000.0
ROPE_DIM = HEAD_DIM // 4
X_HEADS = 4
X_HEAD_DIM = D_MODEL // X_HEADS
D_FF = 2816
NORM_EPS = 1e-6
LN_EPS = 1e-5

LANES = 128
TOKEN_TILE = 512
FF_CHUNK = 512
VMEM_LIMIT_BYTES = 56 * 1024 * 1024


def _rms(x, g):
    return x * lax.rsqrt(jnp.mean(x * x, axis=-1, keepdims=True) + NORM_EPS) * g


def _dot(a, b):
    return jnp.dot(a, b, preferred_element_type=F32)


def _dot_nt(a, b):
    return lax.dot_general(a, b, (((1,), (1,)), ((), ())), preferred_element_type=F32)


def _const_spec(shape):
    zeros = (0,) * len(shape)
    return pl.BlockSpec(shape, lambda *_: zeros, pipeline_mode=pl.Buffered(1))


def _params(n_axes):
    return pltpu.CompilerParams(
        dimension_semantics=("arbitrary",) * n_axes,
        vmem_limit_bytes=VMEM_LIMIT_BYTES,
    )


def _ffn_kernel(x_ref, g_ref, wg_ref, wu_ref, wd_ref, fg_ref, o_ref, *, final_norm):
    x = x_ref[...]
    h = _rms(x, g_ref[...]).astype(BF16)
    acc = jnp.zeros(x.shape, F32)
    for c0 in range(0, D_FF, FF_CHUNK):
        ck = min(FF_CHUNK, D_FF - c0)
        g = _dot(h, wg_ref[:, c0:c0 + ck])
        u = _dot(h, wu_ref[:, c0:c0 + ck])
        a = (g * jax.nn.sigmoid(g)) * u
        acc = acc + _dot(a.astype(BF16), wd_ref[c0:c0 + ck, :])
    y = x + 0.5 * acc
    if final_norm:
        y = _rms(y, fg_ref[...])
    o_ref[...] = y


def _ffn(x, g, wg, wu, wd, fg, *, final_norm):
    t = x.shape[0]
    tile = pl.BlockSpec((TOKEN_TILE, D_MODEL), lambda i: (i, 0))
    return pl.pallas_call(
        functools.partial(_ffn_kernel, final_norm=final_norm),
        grid=(t // TOKEN_TILE,),
        in_specs=[tile, _const_spec((1, D_MODEL)), _const_spec((D_MODEL, D_FF)),
                  _const_spec((D_MODEL, D_FF)), _const_spec((D_FF, D_MODEL)),
                  _const_spec((1, D_MODEL))],
        out_specs=tile,
        out_shape=jax.ShapeDtypeStruct((t, D_MODEL), F32),
        compiler_params=_params(1),
        name="ffn_final" if final_norm else "ffn",
    )(x, g, wg, wu, wd, fg)


def _gelu(x):
    return 0.5 * x * (1.0 + lax.erf(x * (2.0 ** -0.5)))


def _split_bf16(x):
    hi = x.astype(BF16)
    lo = (x - hi.astype(F32)).astype(BF16)
    return hi, lo


def _group_mean(x, gmat):
    hi, lo = _split_bf16(x)
    n = x.shape[0]
    s = _dot(jnp.concatenate([hi, lo], axis=0), gmat)
    return s[:n] + s[n:]


def _rotary(t, cos, sin_signed, first_half):
    parts = []
    for j in range(D_ATT // LANES):
        tj = t[:, j * LANES:(j + 1) * LANES]
        partner = jnp.where(first_half,
                            pltpu.roll(tj, LANES - ROPE_DIM // 2, axis=1),
                            pltpu.roll(tj, ROPE_DIM // 2, axis=1))
        parts.append(tj * cos + partner * sin_signed)
    return jnp.concatenate(parts, axis=1)


def _in_proj_kernel(x_ref, pos_ref, g_ref, win_ref, lng_ref, lnb_ref, ws_ref, bst_ref,
                    gmat_ref, freq_ref, na_ref, ya_ref, q_ref, k_ref, v_ref, ya_scr):
    tm = x_ref.shape[0]
    n_chunks = tm // CHUNK
    h = _rms(x_ref[...], g_ref[...]).astype(BF16)

    def proj(c0, width):
        return _dot(h, win_ref[:, c0:c0 + width])

    u = _gelu(proj(0, D_SGU))
    v = _gelu(proj(D_SGU, D_SGU))
    gmat = gmat_ref[...]
    d = v - _group_mean(v, gmat)
    var = _group_mean(d * d, gmat)
    vln = d * lax.rsqrt(var + LN_EPS) * lng_ref[...] + lnb_ref[...]

    lane = lax.broadcasted_iota(jnp.int32, (CHUNK, LANES), 1)
    low_group = lane < SGU_GROUP_DIM
    row = lax.broadcasted_iota(jnp.int32, (CHUNK, CHUNK), 0)
    col = lax.broadcasted_iota(jnp.int32, (CHUNK, CHUNK), 1)
    causal = col <= row
    bst = bst_ref[...]
    for j in range(D_SGU // LANES):
        g0, g1 = 2 * j, 2 * j + 1
        w0 = jnp.where(causal, ws_ref[g0], 0.0)
        w1 = jnp.where(causal, ws_ref[g1], 0.0)
        wcat = jnp.concatenate([w0, w1], axis=1).astype(BF16)
        vj = vln[:, j * LANES:(j + 1) * LANES]
        top = jnp.concatenate(
            [jnp.where(low_group, vj[c * CHUNK:(c + 1) * CHUNK], 0.0) for c in range(n_chunks)], axis=1)
        bot = jnp.concatenate(
            [jnp.where(low_group, 0.0, vj[c * CHUNK:(c + 1) * CHUNK]) for c in range(n_chunks)], axis=1)
        rhs = jnp.concatenate([top, bot], axis=0).astype(BF16)
        mixed = _dot(wcat, rhs)
        bias = jnp.where(low_group,
                         jnp.broadcast_to(bst[:, g0:g0 + 1], (CHUNK, LANES)),
                         jnp.broadcast_to(bst[:, g1:g1 + 1], (CHUNK, LANES)))
        for c in range(n_chunks):
            uj = u[c * CHUNK:(c + 1) * CHUNK, j * LANES:(j + 1) * LANES]
            ya_scr[c * CHUNK:(c + 1) * CHUNK, j * LANES:(j + 1) * LANES] = (
                uj * (mixed[:, c * LANES:(c + 1) * LANES] + bias))
    ya_ref[...] = _rms(ya_scr[...], na_ref[...]).astype(BF16)

    ang = pos_ref[...].astype(F32) * freq_ref[...]
    cos = jnp.cos(ang)
    sin = jnp.sin(ang)
    lane_t = lax.broadcasted_iota(jnp.int32, (tm, LANES), 1)
    first_half = (lane_t & (HEAD_DIM - 1)) < ROPE_DIM // 2
    sin_signed = jnp.where(first_half, -sin, sin)
    q = proj(2 * D_SGU, D_ATT) * (HEAD_DIM ** -0.5)
    q_ref[...] = _rotary(q, cos, sin_signed, first_half).astype(BF16)
    k = proj(2 * D_SGU + D_ATT, D_ATT)
    k_ref[...] = _rotary(k, cos, sin_signed, first_half).astype(BF16)
    v_ref[...] = proj(2 * D_SGU + 2 * D_ATT, D_ATT).astype(BF16)


def _in_proj(x, pos, g, w_in, ln_g, ln_b, w_s, b_st, gmat, freq, norm_a):
    t = x.shape[0]
    d_in = w_in.shape[1]
    half = pl.BlockSpec((TOKEN_TILE, D_SGU), lambda i: (i, 0))
    out = jax.ShapeDtypeStruct((t, D_SGU), BF16)
    return pl.pallas_call(
        _in_proj_kernel,
        grid=(t // TOKEN_TILE,),
        in_specs=[pl.BlockSpec((TOKEN_TILE, D_MODEL), lambda i: (i, 0)),
                  pl.BlockSpec((TOKEN_TILE, 1), lambda i: (i, 0)),
                  _const_spec((1, D_MODEL)), _const_spec((D_MODEL, d_in)),
                  _const_spec((1, D_SGU)), _const_spec((1, D_SGU)),
                  _const_spec((SGU_GROUPS, CHUNK, CHUNK)), _const_spec((CHUNK, SGU_GROUPS)),
                  _const_spec((D_SGU, D_SGU)), _const_spec((1, LANES)), _const_spec((1, D_SGU))],
        out_specs=[half, half, half, half],
        out_shape=[out, out, out, out],
        scratch_shapes=[pltpu.VMEM((TOKEN_TILE, D_SGU), F32)],
        compiler_params=_params(1),
        name="in_proj",
    )(x, pos, g, w_in, ln_g, ln_b, w_s, b_st, gmat, freq, norm_a)


def _band_attn_kernel(q_ref, k_ref, v_ref, o_ref, lse_ref, *, n_blocks, heads):
    qi = lax.broadcasted_iota(jnp.int32, (BAND, BAND), 0)
    kj = lax.broadcasted_iota(jnp.int32, (BAND, BAND), 1)
    cur_ok = kj <= qi
    prev_ok = kj >= qi
    both_ok = jnp.concatenate([prev_ok, cur_ok], axis=1)

    def attend(q, k, v, ok):
        s = jnp.where(ok, _dot_nt(q, k), -jnp.inf)
        m = jnp.max(s, axis=-1, keepdims=True)
        p = jnp.exp(s - m)
        l = jnp.sum(p, axis=-1, keepdims=True)
        o = _dot(p.astype(BF16), v) / l
        return o, m + jnp.log(l)

    def store(r0, c0, o, lse):
        o_ref[pl.ds(r0, BAND), c0:c0 + HEAD_DIM] = o
        lse_ref[pl.ds(r0, BAND), c0:c0 + HEAD_DIM] = jnp.broadcast_to(lse, (BAND, HEAD_DIM))

    for hh in range(heads):
        c0 = hh * HEAD_DIM
        cols = slice(c0, c0 + HEAD_DIM)
        o, lse = attend(q_ref[0:BAND, cols], k_ref[0:BAND, cols], v_ref[0:BAND, cols], cur_ok)
        store(0, c0, o, lse)

        if n_blocks > 1:
            def body(n, carry, cols=cols, c0=c0):
                r0 = pl.multiple_of(n * BAND, BAND)
                p0 = pl.multiple_of(r0 - BAND, BAND)
                o, lse = attend(q_ref[pl.ds(r0, BAND), cols], k_ref[pl.ds(p0, 2 * BAND), cols],
                                v_ref[pl.ds(p0, 2 * BAND), cols], both_ok)
                store(r0, c0, o, lse)
                return carry

            lax.fori_loop(1, n_blocks, body, 0)


def _band_attn(q, k, v, dilation):
    b, s, _ = q.shape
    rows = s // dilation
    width = dilation * D_ATT
    heads = D_ATT // HEAD_DIM
    view = lambda t: t.reshape(b, rows, width)
    spec = pl.BlockSpec((None, rows, D_ATT), lambda i, j: (i, 0, j))
    out = jax.ShapeDtypeStruct((b, rows, width), F32)
    o, lse = pl.pallas_call(
        functools.partial(_band_attn_kernel, n_blocks=rows // BAND, heads=heads),
        grid=(b, width // D_ATT),
        in_specs=[spec, spec, spec],
        out_specs=[spec, spec],
        out_shape=[out, out],
        compiler_params=_params(2),
        name=f"band_attn_d{dilation}",
    )(view(q), view(k), view(v))
    return o.reshape(b * s, D_ATT), lse.reshape(b * s, D_ATT)


def _out_proj_kernel(x_ref, ya_ref, o1_ref, l1_ref, o2_ref, l2_ref, o3_ref, l3_ref,
                     nb_ref, wo_ref, out_ref):
    l1, l2, l3 = l1_ref[...], l2_ref[...], l3_ref[...]
    m = jnp.maximum(jnp.maximum(l1, l2), l3)
    e1, e2, e3 = jnp.exp(l1 - m), jnp.exp(l2 - m), jnp.exp(l3 - m)
    inv = 1.0 / (e1 + e2 + e3)
    yb = (e1 * inv) * o1_ref[...] + (e2 * inv) * o2_ref[...] + (e3 * inv) * o3_ref[...]
    yb = _rms(yb, nb_ref[...]).astype(BF16)
    y = _dot(ya_ref[...], wo_ref[0:D_SGU, :]) + _dot(yb, wo_ref[D_SGU:D_SGU + D_ATT, :])
    out_ref[...] = x_ref[...] + y


def _out_proj(x, ya, attn, norm_b, w_out):
    t = x.shape[0]
    full = pl.BlockSpec((TOKEN_TILE, D_MODEL), lambda i: (i, 0))
    half = pl.BlockSpec((TOKEN_TILE, D_ATT), lambda i: (i, 0))
    flat = [a for pair in attn for a in pair]
    return pl.pallas_call(
        _out_proj_kernel,
        grid=(t // TOKEN_TILE,),
        in_specs=[full, half] + [half] * 6 + [_const_spec((1, D_ATT)), _const_spec((D_MODEL, D_MODEL))],
        out_specs=full,
        out_shape=jax.ShapeDtypeStruct((t, D_MODEL), F32),
        compiler_params=_params(1),
        name="out_proj",
    )(x, ya, *flat, norm_b, w_out)


def _mem_kv_kernel(m_ref, g_ref, wkt_ref, wv_ref, kt_ref, v_ref):
    mn = _rms(m_ref[...], g_ref[...]).astype(BF16)
    kt_ref[...] = (_dot_nt(wkt_ref[...], mn) * (X_HEAD_DIM ** -0.5)).astype(BF16)
    v_ref[...] = _dot(mn, wv_ref[...]).astype(BF16)


def _mem_kv(mem, g, wk_t, wv):
    b = mem.shape[0]
    return pl.pallas_call(
        _mem_kv_kernel,
        grid=(b,),
        in_specs=[pl.BlockSpec((None, N_MEM, D_MODEL), lambda i: (i, 0, 0)),
                  _const_spec((1, D_MODEL)), _const_spec((D_MODEL, D_MODEL)),
                  _const_spec((D_MODEL, D_MODEL))],
        out_specs=[pl.BlockSpec((None, D_MODEL, N_MEM), lambda i: (i, 0, 0)),
                   pl.BlockSpec((None, N_MEM, D_MODEL), lambda i: (i, 0, 0))],
        out_shape=[jax.ShapeDtypeStruct((b, D_MODEL, N_MEM), BF16),
                   jax.ShapeDtypeStruct((b, N_MEM, D_MODEL), BF16)],
        compiler_params=_params(1),
        name="mem_kv",
    )(mem, g, wk_t, wv)


def _cross_kernel(x_ref, g_ref, wq_ref, kt_ref, v_ref, wo_ref, out_ref):
    x = x_ref[...]
    h = _rms(x, g_ref[...]).astype(BF16)
    q = _dot(h, wq_ref[...]).astype(BF16)
    heads = []
    for hh in range(X_HEADS):
        cols = slice(hh * X_HEAD_DIM, (hh + 1) * X_HEAD_DIM)
        s = _dot(q[:, cols], kt_ref[cols, :])
        p = jnp.exp(s - jnp.max(s, axis=-1, keepdims=True))
        l = jnp.sum(p, axis=-1, keepdims=True)
        heads.append((_dot(p.astype(BF16), v_ref[:, cols]) / l).astype(BF16))
    o = jnp.concatenate(heads, axis=1)
    out_ref[...] = x + _dot(o, wo_ref[...])


def _cross(x, g, wq, kt, v, wo, batch):
    t = x.shape[0]
    per_batch = t // batch // TOKEN_TILE
    full = pl.BlockSpec((TOKEN_TILE, D_MODEL), lambda b, i: (b * per_batch + i, 0))
    return pl.pallas_call(
        _cross_kernel,
        grid=(batch, per_batch),
        in_specs=[full, _const_spec((1, D_MODEL)), _const_spec((D_MODEL, D_MODEL)),
                  pl.BlockSpec((None, D_MODEL, N_MEM), lambda b, i: (b, 0, 0)),
                  pl.BlockSpec((None, N_MEM, D_MODEL), lambda b, i: (b, 0, 0)),
                  _const_spec((D_MODEL, D_MODEL))],
        out_specs=full,
        out_shape=jax.ShapeDtypeStruct((t, D_MODEL), F32),
        compiler_params=_params(2),
        name="cross",
    )(x, g, wq, kt, v, wo)


def _rope_lane_freq():
    half = ROPE_DIM // 2
    inv_freq = ROPE_THETA ** (-2.0 * jnp.arange(half, dtype=F32) / ROPE_DIM)
    e = jnp.arange(LANES) % HEAD_DIM
    return jnp.where(e < ROPE_DIM, inv_freq[e % half], 0.0).astype(F32).reshape(1, LANES)


def _group_mean_matrix():
    g = jnp.arange(D_SGU) // SGU_GROUP_DIM
    return jnp.where(g[:, None] == g[None, :], 1.0 / SGU_GROUP_DIM, 0.0).astype(BF16)


def kernel(x, mem, positions, ffn1_norm, ffn1_w_gate, ffn1_w_up, ffn1_w_down, mix_norm, w_in, sgu_ln_g, sgu_ln_b, sgu_w_s, sgu_b_s, out_norm_a, out_norm_b, w_out, cross_norm, mem_norm, cross_wq, cross_wk, cross_wv, cross_wo, ffn2_norm, ffn2_w_gate, ffn2_w_up, ffn2_w_down, final_norm):
    b, s, d = x.shape
    t = b * s
    row = lambda p: p.reshape(1, -1).astype(F32)
    xt = x.reshape(t, d)
    pos = positions.reshape(t, 1)
    freq = _rope_lane_freq()
    gmat = _group_mean_matrix()
    fin = row(final_norm)
    for l in range(ffn1_norm.shape[0]):
        xt = _ffn(xt, row(ffn1_norm[l]), ffn1_w_gate[l].astype(BF16), ffn1_w_up[l].astype(BF16),
                  ffn1_w_down[l].astype(BF16), fin, final_norm=False)

        ya, q, k, v = _in_proj(xt, pos, row(mix_norm[l]), w_in[l].astype(BF16), row(sgu_ln_g[l]),
                               row(sgu_ln_b[l]), sgu_w_s[l], sgu_b_s[l].T, gmat, freq,
                               row(out_norm_a[l]))
        q, k, v = (a.reshape(b, s, D_ATT) for a in (q, k, v))
        attn = [_band_attn(q, k, v, dil) for dil in DILATIONS]
        xt = _out_proj(xt, ya, attn, row(out_norm_b[l]), w_out[l].astype(BF16))

        kt, vm = _mem_kv(mem, row(mem_norm[l]), cross_wk[l].T.astype(BF16), cross_wv[l].astype(BF16))
        xt = _cross(xt, row(cross_norm[l]), cross_wq[l].astype(BF16), kt, vm,
                    cross_wo[l].astype(BF16), b)

        last = l == ffn1_norm.shape[0] - 1
        xt = _ffn(xt, row(ffn2_norm[l]), ffn2_w_gate[l].astype(BF16), ffn2_w_up[l].astype(BF16),
                  ffn2_w_down[l].astype(BF16), fin, final_norm=last)
    return xt.reshape(b, s, d)
```

```python
import functools

import jax
import jax.numpy as jnp
from jax import lax
from jax.experimental import pallas as pl
from jax.experimental.pallas import tpu as pltpu

F32 = jnp.float32
BF16 = jnp.bfloat16

D_MODEL = 1024
N_MEM = 256
D_SGU = 512
D_ATT = 512
SGU_GROUPS = 8
SGU_GROUP_DIM = D_SGU // SGU_GROUPS
CHUNK = 128
N_HEADS = 8
HEAD_DIM = D_ATT // N_HEADS
BAND = 128
ROPE_THETA = 500000.0
ROPE_DIM = HEAD_DIM // 4
X_HEADS = 4
X_HEAD_DIM = D_MODEL // X_HEADS
D_FF = 2816
NORM_EPS = 1e-6
LN_EPS = 1e-5

LANES = 128
TOKEN_TILE = 512
FF_CHUNK = 512
VMEM_LIMIT_BYTES = 56 * 1024 * 1024


def _rms(x, g):
    return x * lax.rsqrt(jnp.mean(x * x, axis=-1, keepdims=True) + NORM_EPS) * g


def _dot(a, b):
    return jnp.dot(a, b, preferred_element_type=F32)


def _dot_nt(a, b):
    return lax.dot_general(a, b, (((1,), (1,)), ((), ())), preferred_element_type=F32)


def _const_spec(shape):
    zeros = (0,) * len(shape)
    return pl.BlockSpec(shape, lambda *_: zeros, pipeline_mode=pl.Buffered(1))


def _params(n_axes):
    return pltpu.CompilerParams(
        dimension_semantics=("arbitrary",) * n_axes,
        vmem_limit_bytes=VMEM_LIMIT_BYTES,
    )


def _ffn_kernel(x_ref, g_ref, wg_ref, wu_ref, wd_ref, fg_ref, o_ref, *, final_norm):
    x = x_ref[...]
    h = _rms(x, g_ref[...]).astype(BF16)
    acc = jnp.zeros(x.shape, F32)
    for c0 in range(0, D_FF, FF_CHUNK):
        ck = min(FF_CHUNK, D_FF - c0)
        g = _dot(h, wg_ref[:, c0:c0 + ck])
        u = _dot(h, wu_ref[:, c0:c0 + ck])
        a = (g * jax.nn.sigmoid(g)) * u
        acc = acc + _dot(a.astype(BF16), wd_ref[c0:c0 + ck, :])
    y = x + 0.5 * acc
    if final_norm:
        y = _rms(y, fg_ref[...])
    o_ref[...] = y


def _ffn(x, g, wg, wu, wd, fg, *, final_norm):
    t = x.shape[0]
    tile = pl.BlockSpec((TOKEN_TILE, D_MODEL), lambda i: (i, 0))
    return pl.pallas_call(
        functools.partial(_ffn_kernel, final_norm=final_norm),
        grid=(t // TOKEN_TILE,),
        in_specs=[tile, _const_spec((1, D_MODEL)), _const_spec((D_MODEL, D_FF)),
                  _const_spec((D_MODEL, D_FF)), _const_spec((D_FF, D_MODEL)),
                  _const_spec((1, D_MODEL))],
        out_specs=tile,
        out_shape=jax.ShapeDtypeStruct((t, D_MODEL), F32),
        compiler_params=_params(1),
        name="ffn_final" if final_norm else "ffn",
    )(x, g, wg, wu, wd, fg)


def _gelu(x):
    return 0.5 * x * (1.0 + lax.erf(x * (2.0 ** -0.5)))


def _split_bf16(x):
    hi = x.astype(BF16)
    lo = (x - hi.astype(F32)).astype(BF16)
    return hi, lo


def _group_mean(x, gmat):
    hi, lo = _split_bf16(x)
    n = x.shape[0]
    s = _dot(jnp.concatenate([hi, lo], axis=0), gmat)
    return s[:n] + s[n:]


def _rotary(t, cos, sin_signed, first_half):
    parts = []
    for j in range(D_ATT // LANES):
        tj = t[:, j * LANES:(j + 1) * LANES]
        partner = jnp.where(first_half,
                            pltpu.roll(tj, LANES - ROPE_DIM // 2, axis=1),
                            pltpu.roll(tj, ROPE_DIM // 2, axis=1))
        parts.append(tj * cos + partner * sin_signed)
    return jnp.concatenate(parts, axis=1)


def _in_proj_kernel(x_ref, pos_ref, g_ref, win_ref, lng_ref, lnb_ref, ws_ref, bst_ref,
                    gmat_ref, freq_ref, na_ref, ya_ref, q_ref, k_ref, v_ref, ya_scr):
    tm = x_ref.shape[0]
    n_chunks = tm // CHUNK
    h = _rms(x_ref[...], g_ref[...]).astype(BF16)

    def proj(c0, width):
        return _dot(h, win_ref[:, c0:c0 + width])

    u = _gelu(proj(0, D_SGU))
    v = _gelu(proj(D_SGU, D_SGU))
    gmat = gmat_ref[...]
    d = v - _group_mean(v, gmat)
    var = _group_mean(d * d, gmat)
    vln = d * lax.rsqrt(var + LN_EPS) * lng_ref[...] + lnb_ref[...]

    lane = lax.broadcasted_iota(jnp.int32, (CHUNK, LANES), 1)
    low_group = lane < SGU_GROUP_DIM
    row = lax.broadcasted_iota(jnp.int32, (CHUNK, CHUNK), 0)
    col = lax.broadcasted_iota(jnp.int32, (CHUNK, CHUNK), 1)
    causal = col <= row
    bst = bst_ref[...]
    for j in range(D_SGU // LANES):
        g0, g1 = 2 * j, 2 * j + 1
        w0 = jnp.where(causal, ws_ref[g0], 0.0)
        w1 = jnp.where(causal, ws_ref[g1], 0.0)
        wcat = jnp.concatenate([w0, w1], axis=1).astype(BF16)
        vj = vln[:, j * LANES:(j + 1) * LANES]
        top = jnp.concatenate(
            [jnp.where(low_group, vj[c * CHUNK:(c + 1) * CHUNK], 0.0) for c in range(n_chunks)], axis=1)
        bot = jnp.concatenate(
            [jnp.where(low_group, 0.0, vj[c * CHUNK:(c + 1) * CHUNK]) for c in range(n_chunks)], axis=1)
        rhs = jnp.concatenate([top, bot], axis=0).astype(BF16)
        mixed = _dot(wcat, rhs)
        bias = jnp.where(low_group,
                         jnp.broadcast_to(bst[:, g0:g0 + 1], (CHUNK, LANES)),
                         jnp.broadcast_to(bst[:, g1:g1 + 1], (CHUNK, LANES)))
        for c in range(n_chunks):
            uj = u[c * CHUNK:(c + 1) * CHUNK, j * LANES:(j + 1) * LANES]
            ya_scr[c * CHUNK:(c + 1) * CHUNK, j * LANES:(j + 1) * LANES] = (
                uj * (mixed[:, c * LANES:(c + 1) * LANES] + bias))
    ya_ref[...] = _rms(ya_scr[...], na_ref[...]).astype(BF16)

    ang = pos_ref[...].astype(F32) * freq_ref[...]
    cos = jnp.cos(ang)
    sin = jnp.sin(ang)
    lane_t = lax.broadcasted_iota(jnp.int32, (tm, LANES), 1)
    first_half = (lane_t & (HEAD_DIM - 1)) < ROPE_DIM // 2
    sin_signed = jnp.where(first_half, -sin, sin)
    q = proj(2 * D_SGU, D_ATT) * (HEAD_DIM ** -0.5)
    q_ref[...] = _rotary(q, cos, sin_signed, first_half)
    k = proj(2 * D_SGU + D_ATT, D_ATT)
    k_ref[...] = _rotary(k, cos, sin_signed, first_half)
    v_ref[...] = proj(2 * D_SGU + 2 * D_ATT, D_ATT)


def _in_proj(x, pos, g, w_in, ln_g, ln_b, w_s, b_st, gmat, freq, norm_a):
    t = x.shape[0]
    d_in = w_in.shape[1]
    half = pl.BlockSpec((TOKEN_TILE, D_SGU), lambda i: (i, 0))
    return pl.pallas_call(
        _in_proj_kernel,
        grid=(t // TOKEN_TILE,),
        in_specs=[pl.BlockSpec((TOKEN_TILE, D_MODEL), lambda i: (i, 0)),
                  pl.BlockSpec((TOKEN_TILE, 1), lambda i: (i, 0)),
                  _const_spec((1, D_MODEL)), _const_spec((D_MODEL, d_in)),
                  _const_spec((1, D_SGU)), _const_spec((1, D_SGU)),
                  _const_spec((SGU_GROUPS, CHUNK, CHUNK)), _const_spec((CHUNK, SGU_GROUPS)),
                  _const_spec((D_SGU, D_SGU)), _const_spec((1, LANES)), _const_spec((1, D_SGU))],
        out_specs=[half, half, half, half],
        out_shape=[jax.ShapeDtypeStruct((t, D_SGU), BF16)] + [jax.ShapeDtypeStruct((t, D_ATT), F32)] * 3,
        scratch_shapes=[pltpu.VMEM((TOKEN_TILE, D_SGU), F32)],
        compiler_params=_params(1),
        name="in_proj",
    )(x, pos, g, w_in, ln_g, ln_b, w_s, b_st, gmat, freq, norm_a)


Q_LO, Q_HI, K_ALL, V_LO, V_HI = range(5)
CLASS_STRIDE = 16
SUB = BAND // 4


def _merge(o_a, l_a, o_b, l_b):
    m = jnp.maximum(l_a, l_b)
    e_a = jnp.exp(l_a - m)
    e_b = jnp.exp(l_b - m)
    den = e_a + e_b
    return (e_a * o_a + e_b * o_b) / den, m + jnp.log(den)


def _attn_kernel(q_ref, k_ref, v_ref, y_ref, nat, cls, o1, l1, o4, l4, o16, l16, on, ln):
    s_len = q_ref.shape[0]
    n_blk = s_len // BAND
    lane = lax.broadcasted_iota(jnp.int32, (BAND, LANES), 1)
    lo = lane < HEAD_DIM
    ones_lo = jnp.where(lo, 1.0, 0.0).astype(BF16)
    ones_hi = jnp.where(lo, 0.0, 1.0).astype(BF16)

    def fill(dst, blk, src_rows):
        dst_rows = slice(blk * BAND, (blk + 1) * BAND)
        q, k, v = q_ref[src_rows, :], k_ref[src_rows, :], v_ref[src_rows, :]
        dst[Q_LO, dst_rows, :] = jnp.where(lo, q, 0.0).astype(BF16)
        dst[Q_HI, dst_rows, :] = jnp.where(lo, 0.0, q).astype(BF16)
        dst[K_ALL, dst_rows, :] = k.astype(BF16)
        dst[V_LO, dst_rows, :] = jnp.where(lo, v, 0.0).astype(BF16)
        dst[V_HI, dst_rows, :] = jnp.where(lo, 0.0, v).astype(BF16)

    for blk in range(n_blk):
        fill(nat, blk, slice(blk * BAND, (blk + 1) * BAND))
        fill(cls, blk, pl.ds(blk, BAND, stride=CLASS_STRIDE))

    qi = lax.broadcasted_iota(jnp.int32, (2 * BAND, BAND), 0) & (BAND - 1)
    kj = lax.broadcasted_iota(jnp.int32, (2 * BAND, BAND), 1)

    def masks(pos):
        neg = jnp.float32(-jnp.inf)
        prev = jnp.where(pos(kj) >= pos(qi), 0.0, neg)
        cur = jnp.where(pos(kj) <= pos(qi), 0.0, neg)
        return {BAND: cur, 2 * BAND: jnp.concatenate([prev, cur], axis=1)}

    masks1 = masks(lambda i: i)
    sub_shift = SUB.bit_length() - 1
    masks4 = masks(lambda i: 4 * (i & (SUB - 1)) + (i >> sub_shift))

    def block(src, q_rows, k_rows, mask):
        gather = lambda idx, rows: jnp.concatenate([src[idx, r, :] for r in rows], axis=0)
        qs = jnp.concatenate([gather(Q_LO, q_rows), gather(Q_HI, q_rows)], axis=0)
        n_k = sum(r.stop - r.start for r in k_rows)
        s = _dot_nt(qs, gather(K_ALL, k_rows)) + mask[n_k]
        m = jnp.max(s, axis=-1, keepdims=True)
        p = jnp.exp(s - m).astype(BF16)
        p = jnp.concatenate([p[:BAND], p[BAND:]], axis=1)
        reps = n_k // BAND
        rhs = jnp.concatenate(
            [jnp.concatenate([gather(V_LO, k_rows), jnp.concatenate([ones_lo] * reps, axis=0)], axis=1),
             jnp.concatenate([gather(V_HI, k_rows), jnp.concatenate([ones_hi] * reps, axis=0)], axis=1)],
            axis=0)
        r = _dot(p, rhs)
        acc, den = r[:, :LANES], r[:, LANES:]
        return acc / den, jnp.where(lo, m[:BAND], m[BAND:]) + jnp.log(den)

    rows = lambda start, n=BAND: slice(start, start + n)

    for blk in range(n_blk):
        k_rows = [rows(0)] if blk == 0 else [rows((blk - 1) * BAND, 2 * BAND)]
        o, lse = block(nat, [rows(blk * BAND)], k_rows, masks1)
        o1[rows(blk * BAND), :] = o
        l1[rows(blk * BAND), :] = lse

    for c in range(CLASS_STRIDE):
        o, lse = block(cls, [rows(c * BAND)], [rows(c * BAND)], masks1)
        o16[rows(c * BAND), :] = o
        l16[rows(c * BAND), :] = lse

    for r4 in range(4):
        pieces = lambda kb: [rows((r4 + 4 * c) * BAND + kb * SUB, SUB) for c in range(4)]
        for kb in range(BAND // SUB):
            k_rows = pieces(kb) if kb == 0 else pieces(kb - 1) + pieces(kb)
            o, lse = block(cls, pieces(kb), k_rows, masks4)
            for c, piece in enumerate(pieces(kb)):
                o4[piece, :] = o[c * SUB:(c + 1) * SUB]
                l4[piece, :] = lse[c * SUB:(c + 1) * SUB]

    for c in range(CLASS_STRIDE):
        r = rows(c * BAND)
        o, lse = _merge(o4[r, :], l4[r, :], o16[r, :], l16[r, :])
        on[pl.ds(c, BAND, stride=CLASS_STRIDE), :] = o
        ln[pl.ds(c, BAND, stride=CLASS_STRIDE), :] = lse
    for blk in range(n_blk):
        r = rows(blk * BAND)
        y_ref[r, :] = _merge(o1[r, :], l1[r, :], on[r, :], ln[r, :])[0]


def _attn(q, k, v, batch):
    t = q.shape[0]
    s_len = t // batch
    assert s_len == CLASS_STRIDE * BAND and BAND == 4 * SUB
    slab = pl.BlockSpec((s_len, LANES), lambda b, j: (b, j))
    f32_rows = pltpu.VMEM((s_len, LANES), F32)
    return pl.pallas_call(
        _attn_kernel,
        grid=(batch, D_ATT // LANES),
        in_specs=[slab, slab, slab],
        out_specs=slab,
        out_shape=jax.ShapeDtypeStruct((t, D_ATT), F32),
        scratch_shapes=[pltpu.VMEM((5, s_len, LANES), BF16), pltpu.VMEM((5, s_len, LANES), BF16)]
        + [f32_rows] * 8,
        compiler_params=_params(2),
        name="dilated_attn",
    )(q, k, v)


def _out_proj_kernel(x_ref, ya_ref, yb_ref, nb_ref, wo_ref, out_ref):
    yb = _rms(yb_ref[...], nb_ref[...]).astype(BF16)
    y = _dot(ya_ref[...], wo_ref[0:D_SGU, :]) + _dot(yb, wo_ref[D_SGU:D_SGU + D_ATT, :])
    out_ref[...] = x_ref[...] + y


def _out_proj(x, ya, yb, norm_b, w_out):
    t = x.shape[0]
    full = pl.BlockSpec((TOKEN_TILE, D_MODEL), lambda i: (i, 0))
    half = pl.BlockSpec((TOKEN_TILE, D_ATT), lambda i: (i, 0))
    return pl.pallas_call(
        _out_proj_kernel,
        grid=(t // TOKEN_TILE,),
        in_specs=[full, half, half, _const_spec((1, D_ATT)), _const_spec((D_MODEL, D_MODEL))],
        out_specs=full,
        out_shape=jax.ShapeDtypeStruct((t, D_MODEL), F32),
        compiler_params=_params(1),
        name="out_proj",
    )(x, ya, yb, norm_b, w_out)


def _mem_kv_kernel(m_ref, g_ref, wkt_ref, wv_ref, kt_ref, v_ref):
    mn = _rms(m_ref[...], g_ref[...]).astype(BF16)
    kt_ref[...] = (_dot_nt(wkt_ref[...], mn) * (X_HEAD_DIM ** -0.5)).astype(BF16)
    v_ref[...] = _dot(mn, wv_ref[...]).astype(BF16)


def _mem_kv(mem, g, wk_t, wv):
    b = mem.shape[0]
    return pl.pallas_call(
        _mem_kv_kernel,
        grid=(b,),
        in_specs=[pl.BlockSpec((None, N_MEM, D_MODEL), lambda i: (i, 0, 0)),
                  _const_spec((1, D_MODEL)), _const_spec((D_MODEL, D_MODEL)),
                  _const_spec((D_MODEL, D_MODEL))],
        out_specs=[pl.BlockSpec((None, D_MODEL, N_MEM), lambda i: (i, 0, 0)),
                   pl.BlockSpec((None, N_MEM, D_MODEL), lambda i: (i, 0, 0))],
        out_shape=[jax.ShapeDtypeStruct((b, D_MODEL, N_MEM), BF16),
                   jax.ShapeDtypeStruct((b, N_MEM, D_MODEL), BF16)],
        compiler_params=_params(1),
        name="mem_kv",
    )(mem, g, wk_t, wv)


def _cross_kernel(x_ref, g_ref, wq_ref, kt_ref, v_ref, wo_ref, out_ref):
    x = x_ref[...]
    h = _rms(x, g_ref[...]).astype(BF16)
    q = _dot(h, wq_ref[...]).astype(BF16)
    heads = []
    for hh in range(X_HEADS):
        cols = slice(hh * X_HEAD_DIM, (hh + 1) * X_HEAD_DIM)
        s = _dot(q[:, cols], kt_ref[cols, :])
        p = jnp.exp(s - jnp.max(s, axis=-1, keepdims=True))
        l = jnp.sum(p, axis=-1, keepdims=True)
        heads.append((_dot(p.astype(BF16), v_ref[:, cols]) / l).astype(BF16))
    o = jnp.concatenate(heads, axis=1)
    out_ref[...] = x + _dot(o, wo_ref[...])


def _cross(x, g, wq, kt, v, wo, batch):
    t = x.shape[0]
    per_batch = t // batch // TOKEN_TILE
    full = pl.BlockSpec((TOKEN_TILE, D_MODEL), lambda b, i: (b * per_batch + i, 0))
    return pl.pallas_call(
        _cross_kernel,
        grid=(batch, per_batch),
        in_specs=[full, _const_spec((1, D_MODEL)), _const_spec((D_MODEL, D_MODEL)),
                  pl.BlockSpec((None, D_MODEL, N_MEM), lambda b, i: (b, 0, 0)),
                  pl.BlockSpec((None, N_MEM, D_MODEL), lambda b, i: (b, 0, 0)),
                  _const_spec((D_MODEL, D_MODEL))],
        out_specs=full,
        out_shape=jax.ShapeDtypeStruct((t, D_MODEL), F32),
        compiler_params=_params(2),
        name="cross",
    )(x, g, wq, kt, v, wo)


def _rope_lane_freq():
    half = ROPE_DIM // 2
    inv_freq = ROPE_THETA ** (-2.0 * jnp.arange(half, dtype=F32) / ROPE_DIM)
    e = jnp.arange(LANES) % HEAD_DIM
    return jnp.where(e < ROPE_DIM, inv_freq[e % half], 0.0).astype(F32).reshape(1, LANES)


def _group_mean_matrix():
    g = jnp.arange(D_SGU) // SGU_GROUP_DIM
    return jnp.where(g[:, None] == g[None, :], 1.0 / SGU_GROUP_DIM, 0.0).astype(BF16)


def kernel(x, mem, positions, ffn1_norm, ffn1_w_gate, ffn1_w_up, ffn1_w_down, mix_norm, w_in, sgu_ln_g, sgu_ln_b, sgu_w_s, sgu_b_s, out_norm_a, out_norm_b, w_out, cross_norm, mem_norm, cross_wq, cross_wk, cross_wv, cross_wo, ffn2_norm, ffn2_w_gate, ffn2_w_up, ffn2_w_down, final_norm):
    b, s, d = x.shape
    t = b * s
    depth = ffn1_norm.shape[0]
    row = lambda p: p.reshape(1, -1).astype(F32)
    xt = x.reshape(t, d)
    pos = positions.reshape(t, 1)
    freq = _rope_lane_freq()
    gmat = _group_mean_matrix()
    fin = row(final_norm)
    for l in range(depth):
        xt = _ffn(xt, row(ffn1_norm[l]), ffn1_w_gate[l].astype(BF16), ffn1_w_up[l].astype(BF16),
                  ffn1_w_down[l].astype(BF16), fin, final_norm=False)

        ya, q, k, v = _in_proj(xt, pos, row(mix_norm[l]), w_in[l].astype(BF16), row(sgu_ln_g[l]),
                               row(sgu_ln_b[l]), sgu_w_s[l], sgu_b_s[l].T, gmat, freq,
                               row(out_norm_a[l]))
        yb = _attn(q, k, v, b)
        xt = _out_proj(xt, ya, yb, row(out_norm_b[l]), w_out[l].astype(BF16))

        kt, vm = _mem_kv(mem, row(mem_norm[l]), cross_wk[l].T.astype(BF16), cross_wv[l].astype(BF16))
        xt = _cross(xt, row(cross_norm[l]), cross_wq[l].astype(BF16), kt, vm,
                    cross_wo[l].astype(BF16), b)

        xt = _ffn(xt, row(ffn2_norm[l]), ffn2_w_gate[l].astype(BF16), ffn2_w_up[l].astype(BF16),
                  ffn2_w_down[l].astype(BF16), fin, final_norm=(l == depth - 1))
    return xt.reshape(b, s, d)
```

```python
import functools

import jax
import jax.numpy as jnp
from jax import lax
from jax.experimental import pallas as pl
from jax.experimental.pallas import tpu as pltpu

F32 = jnp.float32
BF16 = jnp.bfloat16

D_MODEL = 1024
N_MEM = 256
D_SGU = 512
D_ATT = 512
SGU_GROUPS = 8
SGU_GROUP_DIM = D_SGU // SGU_GROUPS
CHUNK = 128
N_HEADS = 8
HEAD_DIM = D_ATT // N_HEADS
BAND = 128
ROPE_THETA = 500000.0
ROPE_DIM = HEAD_DIM // 4
X_HEADS = 4
X_HEAD_DIM = D_MODEL // X_HEADS
D_FF = 2816
NORM_EPS = 1e-6
LN_EPS = 1e-5
LOG2_E = 1.4426950408889634

LANES = 128
TOKEN_TILE = 512
FF_CHUNK = 512
VMEM_LIMIT_BYTES = 56 * 1024 * 1024


def _rms(x, g):
    return x * lax.rsqrt(jnp.mean(x * x, axis=-1, keepdims=True) + NORM_EPS) * g


def _dot(a, b):
    return jnp.dot(a, b, preferred_element_type=F32)


def _dot_nt(a, b):
    return lax.dot_general(a, b, (((1,), (1,)), ((), ())), preferred_element_type=F32)


def _const_spec(shape):
    zeros = (0,) * len(shape)
    return pl.BlockSpec(shape, lambda *_: zeros, pipeline_mode=pl.Buffered(1))


def _params(n_axes):
    return pltpu.CompilerParams(
        dimension_semantics=("arbitrary",) * n_axes,
        vmem_limit_bytes=VMEM_LIMIT_BYTES,
    )


def _ffn_kernel(x_ref, g_ref, wg_ref, wu_ref, wd_ref, fg_ref, o_ref, *, final_norm):
    x = x_ref[...]
    h = _rms(x, g_ref[...]).astype(BF16)
    acc = jnp.zeros(x.shape, F32)
    for c0 in range(0, D_FF, FF_CHUNK):
        ck = min(FF_CHUNK, D_FF - c0)
        g = _dot(h, wg_ref[:, c0:c0 + ck])
        u = _dot(h, wu_ref[:, c0:c0 + ck])
        a = (g * jax.nn.sigmoid(g)) * u
        acc = acc + _dot(a.astype(BF16), wd_ref[c0:c0 + ck, :])
    y = x + 0.5 * acc
    if final_norm:
        y = _rms(y, fg_ref[...])
    o_ref[...] = y


def _ffn(x, g, wg, wu, wd, fg, *, final_norm):
    t = x.shape[0]
    tile = pl.BlockSpec((TOKEN_TILE, D_MODEL), lambda i: (i, 0))
    return pl.pallas_call(
        functools.partial(_ffn_kernel, final_norm=final_norm),
        grid=(t // TOKEN_TILE,),
        in_specs=[tile, _const_spec((1, D_MODEL)), _const_spec((D_MODEL, D_FF)),
                  _const_spec((D_MODEL, D_FF)), _const_spec((D_FF, D_MODEL)),
                  _const_spec((1, D_MODEL))],
        out_specs=tile,
        out_shape=jax.ShapeDtypeStruct((t, D_MODEL), F32),
        compiler_params=_params(1),
        name="ffn_final" if final_norm else "ffn",
    )(x, g, wg, wu, wd, fg)


def _gelu(x):
    return 0.5 * x * (1.0 + lax.erf(x * (2.0 ** -0.5)))


def _split_bf16(x):
    hi = x.astype(BF16)
    lo = (x - hi.astype(F32)).astype(BF16)
    return hi, lo


def _group_mean(x, gmat):
    hi, lo = _split_bf16(x)
    n = x.shape[0]
    s = _dot(jnp.concatenate([hi, lo], axis=0), gmat)
    return s[:n] + s[n:]


def _rotary(t, cos, sin_signed, first_half):
    parts = []
    for j in range(D_ATT // LANES):
        tj = t[:, j * LANES:(j + 1) * LANES]
        partner = jnp.where(first_half,
                            pltpu.roll(tj, LANES - ROPE_DIM // 2, axis=1),
                            pltpu.roll(tj, ROPE_DIM // 2, axis=1))
        parts.append(tj * cos + partner * sin_signed)
    return jnp.concatenate(parts, axis=1)


def _in_proj_kernel(x_ref, pos_ref, g_ref, win_ref, lng_ref, lnb_ref, ws_ref, bst_ref,
                    gmat_ref, freq_ref, na_ref, ya_ref, q_ref, k_ref, v_ref, ya_scr):
    tm = x_ref.shape[0]
    n_chunks = tm // CHUNK
    h = _rms(x_ref[...], g_ref[...]).astype(BF16)

    def proj(c0, width):
        return _dot(h, win_ref[:, c0:c0 + width])

    u = _gelu(proj(0, D_SGU))
    v = _gelu(proj(D_SGU, D_SGU))
    gmat = gmat_ref[...]
    d = v - _group_mean(v, gmat)
    var = _group_mean(d * d, gmat)
    vln = d * lax.rsqrt(var + LN_EPS) * lng_ref[...] + lnb_ref[...]

    lane = lax.broadcasted_iota(jnp.int32, (CHUNK, LANES), 1)
    low_group = lane < SGU_GROUP_DIM
    row = lax.broadcasted_iota(jnp.int32, (CHUNK, CHUNK), 0)
    col = lax.broadcasted_iota(jnp.int32, (CHUNK, CHUNK), 1)
    causal = col <= row
    bst = bst_ref[...]
    for j in range(D_SGU // LANES):
        g0, g1 = 2 * j, 2 * j + 1
        w0 = jnp.where(causal, ws_ref[g0], 0.0)
        w1 = jnp.where(causal, ws_ref[g1], 0.0)
        wcat = jnp.concatenate([w0, w1], axis=1).astype(BF16)
        vj = vln[:, j * LANES:(j + 1) * LANES]
        top = jnp.concatenate(
            [jnp.where(low_group, vj[c * CHUNK:(c + 1) * CHUNK], 0.0) for c in range(n_chunks)], axis=1)
        bot = jnp.concatenate(
            [jnp.where(low_group, 0.0, vj[c * CHUNK:(c + 1) * CHUNK]) for c in range(n_chunks)], axis=1)
        rhs = jnp.concatenate([top, bot], axis=0).astype(BF16)
        mixed = _dot(wcat, rhs)
        bias = jnp.where(low_group,
                         jnp.broadcast_to(bst[:, g0:g0 + 1], (CHUNK, LANES)),
                         jnp.broadcast_to(bst[:, g1:g1 + 1], (CHUNK, LANES)))
        for c in range(n_chunks):
            uj = u[c * CHUNK:(c + 1) * CHUNK, j * LANES:(j + 1) * LANES]
            ya_scr[c * CHUNK:(c + 1) * CHUNK, j * LANES:(j + 1) * LANES] = (
                uj * (mixed[:, c * LANES:(c + 1) * LANES] + bias))
    ya_ref[...] = _rms(ya_scr[...], na_ref[...]).astype(BF16)

    ang = pos_ref[...].astype(F32) * freq_ref[...]
    cos = jnp.cos(ang)
    sin = jnp.sin(ang)
    lane_t = lax.broadcasted_iota(jnp.int32, (tm, LANES), 1)
    first_half = (lane_t & (HEAD_DIM - 1)) < ROPE_DIM // 2
    sin_signed = jnp.where(first_half, -sin, sin)
    q = proj(2 * D_SGU, D_ATT) * (HEAD_DIM ** -0.5 * LOG2_E)
    q_ref[...] = _rotary(q, cos, sin_signed, first_half)
    k = proj(2 * D_SGU + D_ATT, D_ATT)
    k_ref[...] = _rotary(k, cos, sin_signed, first_half)
    v_ref[...] = proj(2 * D_SGU + 2 * D_ATT, D_ATT)


def _in_proj(x, pos, g, w_in, ln_g, ln_b, w_s, b_st, gmat, freq, norm_a):
    t = x.shape[0]
    d_in = w_in.shape[1]
    half = pl.BlockSpec((TOKEN_TILE, D_SGU), lambda i: (i, 0))
    return pl.pallas_call(
        _in_proj_kernel,
        grid=(t // TOKEN_TILE,),
        in_specs=[pl.BlockSpec((TOKEN_TILE, D_MODEL), lambda i: (i, 0)),
                  pl.BlockSpec((TOKEN_TILE, 1), lambda i: (i, 0)),
                  _const_spec((1, D_MODEL)), _const_spec((D_MODEL, d_in)),
                  _const_spec((1, D_SGU)), _const_spec((1, D_SGU)),
                  _const_spec((SGU_GROUPS, CHUNK, CHUNK)), _const_spec((CHUNK, SGU_GROUPS)),
                  _const_spec((D_SGU, D_SGU)), _const_spec((1, LANES)), _const_spec((1, D_SGU))],
        out_specs=[half, half, half, half],
        out_shape=[jax.ShapeDtypeStruct((t, D_SGU), BF16)] + [jax.ShapeDtypeStruct((t, D_ATT), F32)] * 3,
        scratch_shapes=[pltpu.VMEM((TOKEN_TILE, D_SGU), F32)],
        compiler_params=_params(1),
        name="in_proj",
    )(x, pos, g, w_in, ln_g, ln_b, w_s, b_st, gmat, freq, norm_a)


Q_LO, Q_HI, K_ALL, V_LO, V_HI = range(5)
ACC, DEN, MAX = range(3)
SUB = BAND // 4


def _merge(a, b):
    m = jnp.maximum(a[MAX], b[MAX])
    w_a = jnp.exp2(a[MAX] - m)
    w_b = jnp.exp2(b[MAX] - m)
    return w_a * a[ACC] + w_b * b[ACC], w_a * a[DEN] + w_b * b[DEN], m


def _attn_kernel(q_ref, k_ref, v_ref, y_ref, nat, cls, tmp, st1, st4, st16, stn):
    s_len = q_ref.shape[0]
    n_blk = s_len // BAND
    quarter = s_len // 4
    rows = lambda start, n=BAND: slice(start, start + n)
    lane = lax.broadcasted_iota(jnp.int32, (BAND, LANES), 1)
    lo = lane < HEAD_DIM
    ones_lo = jnp.where(lo, 1.0, 0.0).astype(BF16)
    ones_hi = jnp.where(lo, 0.0, 1.0).astype(BF16)

    def fill(dst, blk, qkv):
        q, k, v = qkv
        dst_rows = rows(blk * BAND)
        dst[Q_LO, dst_rows, :] = jnp.where(lo, q, 0.0).astype(BF16)
        dst[Q_HI, dst_rows, :] = jnp.where(lo, 0.0, q).astype(BF16)
        dst[K_ALL, dst_rows, :] = k.astype(BF16)
        dst[V_LO, dst_rows, :] = jnp.where(lo, v, 0.0).astype(BF16)
        dst[V_HI, dst_rows, :] = jnp.where(lo, 0.0, v).astype(BF16)

    srcs = (q_ref, k_ref, v_ref)
    for r4 in range(4):
        for kb in range(quarter // BAND):
            for i, src in enumerate(srcs):
                tmp[i, rows(r4 * quarter + kb * BAND), :] = src[pl.ds(r4 + 4 * BAND * kb, BAND, stride=4), :]
    for blk in range(n_blk):
        fill(nat, blk, [src[rows(blk * BAND), :] for src in srcs])
    for c16 in range(n_blk):
        r4, c = c16 % 4, c16 // 4
        fill(cls, c16, [tmp[i, pl.ds(r4 * quarter + c, BAND, stride=4), :] for i in range(3)])

    qi = lax.broadcasted_iota(jnp.int32, (2 * BAND, BAND), 0) & (BAND - 1)
    kj = lax.broadcasted_iota(jnp.int32, (2 * BAND, BAND), 1)

    def masks(pos):
        neg = jnp.float32(-jnp.inf)
        prev = jnp.where(pos(kj) >= pos(qi), 0.0, neg)
        cur = jnp.where(pos(kj) <= pos(qi), 0.0, neg)
        return {BAND: cur, 2 * BAND: jnp.concatenate([prev, cur], axis=1)}

    masks1 = masks(lambda i: i)
    sub_shift = SUB.bit_length() - 1
    masks4 = masks(lambda i: 4 * (i & (SUB - 1)) + (i >> sub_shift))

    def block(src, q_rows, k_rows, mask):
        gather = lambda idx, rws: jnp.concatenate([src[idx, r, :] for r in rws], axis=0)
        qs = jnp.concatenate([gather(Q_LO, q_rows), gather(Q_HI, q_rows)], axis=0)
        n_k = sum(r.stop - r.start for r in k_rows)
        s = _dot_nt(qs, gather(K_ALL, k_rows)) + mask[n_k]
        m = jnp.max(s, axis=-1, keepdims=True)
        p = jnp.exp2(s - m).astype(BF16)
        p = jnp.concatenate([p[:BAND], p[BAND:]], axis=1)
        reps = n_k // BAND
        rhs = jnp.concatenate(
            [jnp.concatenate([gather(V_LO, k_rows), jnp.concatenate([ones_lo] * reps, axis=0)], axis=1),
             jnp.concatenate([gather(V_HI, k_rows), jnp.concatenate([ones_hi] * reps, axis=0)], axis=1)],
            axis=0)
        r = _dot(p, rhs)
        return r[:, :LANES], r[:, LANES:], jnp.where(lo, m[:BAND], m[BAND:])

    def put(st, dst_rows, state, src_rows=slice(None)):
        for i in range(3):
            st[i, dst_rows, :] = state[i][src_rows]

    get = lambda st, r: [st[i, r, :] for i in range(3)]

    for blk in range(n_blk):
        k_rows = [rows(0)] if blk == 0 else [rows((blk - 1) * BAND, 2 * BAND)]
        put(st1, rows(blk * BAND), block(nat, [rows(blk * BAND)], k_rows, masks1))

    for c16 in range(n_blk):
        put(st16, rows(c16 * BAND), block(cls, [rows(c16 * BAND)], [rows(c16 * BAND)], masks1))

    for r4 in range(4):
        pieces = lambda kb: [rows((r4 + 4 * c) * BAND + kb * SUB, SUB) for c in range(4)]
        for kb in range(BAND // SUB):
            k_rows = pieces(kb) if kb == 0 else pieces(kb - 1) + pieces(kb)
            state = block(cls, pieces(kb), k_rows, masks4)
            for c, piece in enumerate(pieces(kb)):
                put(st4, piece, state, rows(c * SUB, SUB))

    for c16 in range(n_blk):
        r4, c = c16 % 4, c16 // 4
        put(tmp, pl.ds(r4 * quarter + c, BAND, stride=4),
            _merge(get(st4, rows(c16 * BAND)), get(st16, rows(c16 * BAND))))
    for r4 in range(4):
        for kb in range(quarter // BAND):
            put(stn, pl.ds(r4 + 4 * BAND * kb, BAND, stride=4), get(tmp, rows(r4 * quarter + kb * BAND)))
    for blk in range(n_blk):
        r = rows(blk * BAND)
        acc, den, _ = _merge(get(st1, r), get(stn, r))
        y_ref[r, :] = acc / den


def _attn(q, k, v, batch):
    t = q.shape[0]
    s_len = t // batch
    assert s_len == 16 * BAND and BAND == 4 * SUB
    slab = pl.BlockSpec((s_len, LANES), lambda b, j: (b, j))
    operands = pltpu.VMEM((5, s_len, LANES), BF16)
    state = pltpu.VMEM((3, s_len, LANES), F32)
    return pl.pallas_call(
        _attn_kernel,
        grid=(batch, D_ATT // LANES),
        in_specs=[slab, slab, slab],
        out_specs=slab,
        out_shape=jax.ShapeDtypeStruct((t, D_ATT), F32),
        scratch_shapes=[operands, operands] + [state] * 5,
        compiler_params=_params(2),
        name="dilated_attn",
    )(q, k, v)


def _mem_kv_kernel(m_ref, g_ref, wkt_ref, wv_ref, kt_ref, v_ref):
    mn = _rms(m_ref[...], g_ref[...]).astype(BF16)
    kt_ref[...] = (_dot_nt(wkt_ref[...], mn) * (X_HEAD_DIM ** -0.5)).astype(BF16)
    v_ref[...] = _dot(mn, wv_ref[...]).astype(BF16)


def _mem_kv(mem, g, wk_t, wv):
    b = mem.shape[0]
    return pl.pallas_call(
        _mem_kv_kernel,
        grid=(b,),
        in_specs=[pl.BlockSpec((None, N_MEM, D_MODEL), lambda i: (i, 0, 0)),
                  _const_spec((1, D_MODEL)), _const_spec((D_MODEL, D_MODEL)),
                  _const_spec((D_MODEL, D_MODEL))],
        out_specs=[pl.BlockSpec((None, D_MODEL, N_MEM), lambda i: (i, 0, 0)),
                   pl.BlockSpec((None, N_MEM, D_MODEL), lambda i: (i, 0, 0))],
        out_shape=[jax.ShapeDtypeStruct((b, D_MODEL, N_MEM), BF16),
                   jax.ShapeDtypeStruct((b, N_MEM, D_MODEL), BF16)],
        compiler_params=_params(1),
        name="mem_kv",
    )(mem, g, wk_t, wv)


def _cross_kernel(x_ref, ya_ref, yb_ref, nb_ref, wmix_ref, g_ref, wq_ref, kt_ref, v_ref, wo_ref, out_ref):
    yb = _rms(yb_ref[...], nb_ref[...]).astype(BF16)
    x = (x_ref[...] + _dot(ya_ref[...], wmix_ref[0:D_SGU, :])
         + _dot(yb, wmix_ref[D_SGU:D_SGU + D_ATT, :]))
    h = _rms(x, g_ref[...]).astype(BF16)
    q = _dot(h, wq_ref[...]).astype(BF16)
    heads = []
    for hh in range(X_HEADS):
        cols = slice(hh * X_HEAD_DIM, (hh + 1) * X_HEAD_DIM)
        s = _dot(q[:, cols], kt_ref[cols, :])
        p = jnp.exp(s - jnp.max(s, axis=-1, keepdims=True))
        l = jnp.sum(p, axis=-1, keepdims=True)
        heads.append((_dot(p.astype(BF16), v_ref[:, cols]) / l).astype(BF16))
    o = jnp.concatenate(heads, axis=1)
    out_ref[...] = x + _dot(o, wo_ref[...])


def _cross(x, ya, yb, norm_b, w_mix, g, wq, kt, v, wo, batch):
    t = x.shape[0]
    per_batch = t // batch // TOKEN_TILE
    full = pl.BlockSpec((TOKEN_TILE, D_MODEL), lambda b, i: (b * per_batch + i, 0))
    half = pl.BlockSpec((TOKEN_TILE, D_ATT), lambda b, i: (b * per_batch + i, 0))
    return pl.pallas_call(
        _cross_kernel,
        grid=(batch, per_batch),
        in_specs=[full, half, half, _const_spec((1, D_ATT)), _const_spec((D_MODEL, D_MODEL)),
                  _const_spec((1, D_MODEL)), _const_spec((D_MODEL, D_MODEL)),
                  pl.BlockSpec((None, D_MODEL, N_MEM), lambda b, i: (b, 0, 0)),
                  pl.BlockSpec((None, N_MEM, D_MODEL), lambda b, i: (b, 0, 0)),
                  _const_spec((D_MODEL, D_MODEL))],
        out_specs=full,
        out_shape=jax.ShapeDtypeStruct((t, D_MODEL), F32),
        compiler_params=_params(2),
        name="mix_out_cross",
    )(x, ya, yb, norm_b, w_mix, g, wq, kt, v, wo)


def _rope_lane_freq():
    half = ROPE_DIM // 2
    inv_freq = ROPE_THETA ** (-2.0 * jnp.arange(half, dtype=F32) / ROPE_DIM)
    e = jnp.arange(LANES) % HEAD_DIM
    return jnp.where(e < ROPE_DIM, inv_freq[e % half], 0.0).astype(F32).reshape(1, LANES)


def _group_mean_matrix():
    g = jnp.arange(D_SGU) // SGU_GROUP_DIM
    return jnp.where(g[:, None] == g[None, :], 1.0 / SGU_GROUP_DIM, 0.0).astype(BF16)


def kernel(x, mem, positions, ffn1_norm, ffn1_w_gate, ffn1_w_up, ffn1_w_down, mix_norm, w_in, sgu_ln_g, sgu_ln_b, sgu_w_s, sgu_b_s, out_norm_a, out_norm_b, w_out, cross_norm, mem_norm, cross_wq, cross_wk, cross_wv, cross_wo, ffn2_norm, ffn2_w_gate, ffn2_w_up, ffn2_w_down, final_norm):
    b, s, d = x.shape
    t = b * s
    depth = ffn1_norm.shape[0]
    row = lambda p: p.reshape(1, -1).astype(F32)
    xt = x.reshape(t, d)
    pos = positions.reshape(t, 1)
    freq = _rope_lane_freq()
    gmat = _group_mean_matrix()
    fin = row(final_norm)
    for l in range(depth):
        xt = _ffn(xt, row(ffn1_norm[l]), ffn1_w_gate[l].astype(BF16), ffn1_w_up[l].astype(BF16),
                  ffn1_w_down[l].astype(BF16), fin, final_norm=False)

        ya, q, k, v = _in_proj(xt, pos, row(mix_norm[l]), w_in[l].astype(BF16), row(sgu_ln_g[l]),
                               row(sgu_ln_b[l]), sgu_w_s[l], sgu_b_s[l].T, gmat, freq,
                               row(out_norm_a[l]))
        yb = _attn(q, k, v, b)
        kt, vm = _mem_kv(mem, row(mem_norm[l]), cross_wk[l].T.astype(BF16), cross_wv[l].astype(BF16))
        xt = _cross(xt, ya, yb, row(out_norm_b[l]), w_out[l].astype(BF16), row(cross_norm[l]),
                    cross_wq[l].astype(BF16), kt, vm, cross_wo[l].astype(BF16), b)

        xt = _ffn(xt, row(ffn2_norm[l]), ffn2_w_gate[l].astype(BF16), ffn2_w_up[l].astype(BF16),
                  ffn2_w_down[l].astype(BF16), fin, final_norm=(l == depth - 1))
    return xt.reshape(b, s, d)
```

```python
import functools

import jax
import jax.numpy as jnp
from jax import lax
from jax.experimental import pallas as pl
from jax.experimental.pallas import tpu as pltpu

F32 = jnp.float32
BF16 = jnp.bfloat16

D_MODEL = 1024
N_MEM = 256
D_SGU = 512
D_ATT = 512
SGU_GROUPS = 8
SGU_GROUP_DIM = D_SGU // SGU_GROUPS
CHUNK = 128
N_HEADS = 8
HEAD_DIM = D_ATT // N_HEADS
BAND = 128
ROPE_THETA = 500000.0
ROPE_DIM = HEAD_DIM // 4
X_HEADS = 4
X_HEAD_DIM = D_MODEL // X_HEADS
D_FF = 2816
NORM_EPS = 1e-6
LN_EPS = 1e-5
LOG2_E = 1.4426950408889634

LANES = 128
TOKEN_TILE = 512
FFN_TILE = 1024
FF_CHUNK = 512
VMEM_LIMIT_BYTES = 56 * 1024 * 1024


def _rms(x, g):
    return x * lax.rsqrt(jnp.mean(x * x, axis=-1, keepdims=True) + NORM_EPS) * g


def _dot(a, b):
    return jnp.dot(a, b, preferred_element_type=F32)


def _dot_nt(a, b):
    return lax.dot_general(a, b, (((1,), (1,)), ((), ())), preferred_element_type=F32)


def _const_spec(shape):
    zeros = (0,) * len(shape)
    return pl.BlockSpec(shape, lambda *_: zeros, pipeline_mode=pl.Buffered(1))


def _params(n_axes, **extra):
    return pltpu.CompilerParams(
        dimension_semantics=("arbitrary",) * n_axes,
        vmem_limit_bytes=VMEM_LIMIT_BYTES,
        **extra,
    )


def _ffn_kernel(x_ref, g_ref, wg_ref, wu_ref, wd_ref, fg_ref, o_ref, *, final_norm):
    x = x_ref[...]
    h = _rms(x, g_ref[...]).astype(BF16)
    acc = jnp.zeros(x.shape, F32)
    for c0 in range(0, D_FF, FF_CHUNK):
        ck = min(FF_CHUNK, D_FF - c0)
        g = _dot(h, wg_ref[:, c0:c0 + ck])
        u = _dot(h, wu_ref[:, c0:c0 + ck])
        a = (g * jax.nn.sigmoid(g)) * u
        acc = acc + _dot(a.astype(BF16), wd_ref[c0:c0 + ck, :])
    y = x + 0.5 * acc
    if final_norm:
        y = _rms(y, fg_ref[...])
    o_ref[...] = y


def _ffn(x, g, wg, wu, wd, fg, *, final_norm):
    t = x.shape[0]
    tile = pl.BlockSpec((FFN_TILE, D_MODEL), lambda i: (i, 0))
    return pl.pallas_call(
        functools.partial(_ffn_kernel, final_norm=final_norm),
        grid=(t // FFN_TILE,),
        in_specs=[tile, _const_spec((1, D_MODEL)), _const_spec((D_MODEL, D_FF)),
                  _const_spec((D_MODEL, D_FF)), _const_spec((D_FF, D_MODEL)),
                  _const_spec((1, D_MODEL))],
        out_specs=tile,
        out_shape=jax.ShapeDtypeStruct((t, D_MODEL), F32),
        compiler_params=_params(1),
        name="ffn_final" if final_norm else "ffn",
    )(x, g, wg, wu, wd, fg)


def _gelu(x):
    return 0.5 * x * (1.0 + lax.erf(x * (2.0 ** -0.5)))


def _group_mean(x, gmat):
    return _dot(x.astype(BF16), gmat)


def _rotary(t, cos, sin_signed, first_half):
    parts = []
    for j in range(D_ATT // LANES):
        tj = t[:, j * LANES:(j + 1) * LANES]
        partner = jnp.where(first_half,
                            pltpu.roll(tj, LANES - ROPE_DIM // 2, axis=1),
                            pltpu.roll(tj, ROPE_DIM // 2, axis=1))
        parts.append(tj * cos + partner * sin_signed)
    return jnp.concatenate(parts, axis=1)


def _in_proj_kernel(x_ref, pos_ref, g_ref, win_ref, lng_ref, lnb_ref, ws_ref, bst_ref,
                    gmat_ref, freq_ref, na_ref, ya_ref, q_ref, k_ref, v_ref, ya_scr):
    tm = x_ref.shape[0]
    n_chunks = tm // CHUNK
    h = _rms(x_ref[...], g_ref[...]).astype(BF16)

    def proj(c0, width):
        return _dot(h, win_ref[:, c0:c0 + width])

    u = _gelu(proj(0, D_SGU))
    v = _gelu(proj(D_SGU, D_SGU))
    gmat = gmat_ref[...]
    d = v - _group_mean(v, gmat)
    var = _group_mean(d * d, gmat)
    vln = d * lax.rsqrt(var + LN_EPS) * lng_ref[...] + lnb_ref[...]

    lane = lax.broadcasted_iota(jnp.int32, (CHUNK, LANES), 1)
    low_group = lane < SGU_GROUP_DIM
    row = lax.broadcasted_iota(jnp.int32, (CHUNK, CHUNK), 0)
    col = lax.broadcasted_iota(jnp.int32, (CHUNK, CHUNK), 1)
    causal = col <= row
    bst = bst_ref[...]
    for j in range(D_SGU // LANES):
        g0, g1 = 2 * j, 2 * j + 1
        w0 = jnp.where(causal, ws_ref[g0], 0.0)
        w1 = jnp.where(causal, ws_ref[g1], 0.0)
        wcat = jnp.concatenate([w0, w1], axis=1).astype(BF16)
        vj = vln[:, j * LANES:(j + 1) * LANES]
        top = jnp.concatenate(
            [jnp.where(low_group, vj[c * CHUNK:(c + 1) * CHUNK], 0.0) for c in range(n_chunks)], axis=1)
        bot = jnp.concatenate(
            [jnp.where(low_group, 0.0, vj[c * CHUNK:(c + 1) * CHUNK]) for c in range(n_chunks)], axis=1)
        rhs = jnp.concatenate([top, bot], axis=0).astype(BF16)
        mixed = _dot(wcat, rhs)
        bias = jnp.where(low_group,
                         jnp.broadcast_to(bst[:, g0:g0 + 1], (CHUNK, LANES)),
                         jnp.broadcast_to(bst[:, g1:g1 + 1], (CHUNK, LANES)))
        for c in range(n_chunks):
            uj = u[c * CHUNK:(c + 1) * CHUNK, j * LANES:(j + 1) * LANES]
            ya_scr[c * CHUNK:(c + 1) * CHUNK, j * LANES:(j + 1) * LANES] = (
                uj * (mixed[:, c * LANES:(c + 1) * LANES] + bias))
    ya_ref[...] = _rms(ya_scr[...], na_ref[...]).astype(BF16)

    ang = pos_ref[...].astype(F32) * freq_ref[...]
    cos = jnp.cos(ang)
    sin = jnp.sin(ang)
    lane_t = lax.broadcasted_iota(jnp.int32, (tm, LANES), 1)
    first_half = (lane_t & (HEAD_DIM - 1)) < ROPE_DIM // 2
    sin_signed = jnp.where(first_half, -sin, sin)
    q = proj(2 * D_SGU, D_ATT) * (HEAD_DIM ** -0.5 * LOG2_E)
    q_ref[...] = _rotary(q, cos, sin_signed, first_half)
    k = proj(2 * D_SGU + D_ATT, D_ATT)
    k_ref[...] = _rotary(k, cos, sin_signed, first_half)
    v_ref[...] = proj(2 * D_SGU + 2 * D_ATT, D_ATT)


def _in_proj(x, pos, g, w_in, ln_g, ln_b, w_s, b_st, gmat, freq, norm_a):
    t = x.shape[0]
    d_in = w_in.shape[1]
    half = pl.BlockSpec((TOKEN_TILE, D_SGU), lambda i: (i, 0))
    return pl.pallas_call(
        _in_proj_kernel,
        grid=(t // TOKEN_TILE,),
        in_specs=[pl.BlockSpec((TOKEN_TILE, D_MODEL), lambda i: (i, 0)),
                  pl.BlockSpec((TOKEN_TILE, 1), lambda i: (i, 0)),
                  _const_spec((1, D_MODEL)), _const_spec((D_MODEL, d_in)),
                  _const_spec((1, D_SGU)), _const_spec((1, D_SGU)),
                  _const_spec((SGU_GROUPS, CHUNK, CHUNK)), _const_spec((CHUNK, SGU_GROUPS)),
                  _const_spec((D_SGU, D_SGU)), _const_spec((1, LANES)), _const_spec((1, D_SGU))],
        out_specs=[half, half, half, half],
        out_shape=[jax.ShapeDtypeStruct((t, D_SGU), BF16)] + [jax.ShapeDtypeStruct((t, D_ATT), F32)] * 3,
        scratch_shapes=[pltpu.VMEM((TOKEN_TILE, D_SGU), F32)],
        compiler_params=_params(1),
        name="in_proj",
    )(x, pos, g, w_in, ln_g, ln_b, w_s, b_st, gmat, freq, norm_a)


Q_LO, Q_HI, K_ALL, V_LO, V_HI = range(5)
ACC, DEN, MAX = range(3)
SUB = BAND // 4


def _merge(a, b):
    m = jnp.maximum(a[MAX], b[MAX])
    w_a = jnp.exp2(a[MAX] - m)
    w_b = jnp.exp2(b[MAX] - m)
    return w_a * a[ACC] + w_b * b[ACC], w_a * a[DEN] + w_b * b[DEN], m


def _attn_kernel(q_ref, k_ref, v_ref, y_ref, nat, cls, tmp, st1, st4, st16, stn):
    s_len = q_ref.shape[0]
    n_blk = s_len // BAND
    quarter = s_len // 4
    rows = lambda start, n=BAND: slice(start, start + n)
    lane = lax.broadcasted_iota(jnp.int32, (BAND, LANES), 1)
    lo = lane < HEAD_DIM
    ones_lo = jnp.where(lo, 1.0, 0.0).astype(BF16)
    ones_hi = jnp.where(lo, 0.0, 1.0).astype(BF16)

    def fill(dst, blk, qkv):
        q, k, v = qkv
        dst_rows = rows(blk * BAND)
        dst[Q_LO, dst_rows, :] = jnp.where(lo, q, 0.0).astype(BF16)
        dst[Q_HI, dst_rows, :] = jnp.where(lo, 0.0, q).astype(BF16)
        dst[K_ALL, dst_rows, :] = k.astype(BF16)
        dst[V_LO, dst_rows, :] = jnp.where(lo, v, 0.0).astype(BF16)
        dst[V_HI, dst_rows, :] = jnp.where(lo, 0.0, v).astype(BF16)

    srcs = (q_ref, k_ref, v_ref)
    for r4 in range(4):
        for kb in range(quarter // BAND):
            for i, src in enumerate(srcs):
                tmp[i, rows(r4 * quarter + kb * BAND), :] = src[pl.ds(r4 + 4 * BAND * kb, BAND, stride=4), :]
    for blk in range(n_blk):
        fill(nat, blk, [src[rows(blk * BAND), :] for src in srcs])
    for c16 in range(n_blk):
        r4, c = c16 % 4, c16 // 4
        fill(cls, c16, [tmp[i, pl.ds(r4 * quarter + c, BAND, stride=4), :] for i in range(3)])

    qi = lax.broadcasted_iota(jnp.int32, (2 * BAND, BAND), 0) & (BAND - 1)
    kj = lax.broadcasted_iota(jnp.int32, (2 * BAND, BAND), 1)

    def masks(pos):
        neg = jnp.float32(-jnp.inf)
        prev = jnp.where(pos(kj) >= pos(qi), 0.0, neg)
        cur = jnp.where(pos(kj) <= pos(qi), 0.0, neg)
        return {BAND: cur, 2 * BAND: jnp.concatenate([prev, cur], axis=1)}

    masks1 = masks(lambda i: i)
    sub_shift = SUB.bit_length() - 1
    masks4 = masks(lambda i: 4 * (i & (SUB - 1)) + (i >> sub_shift))

    def block(src, q_rows, k_rows, mask):
        gather = lambda idx, rws: jnp.concatenate([src[idx, r, :] for r in rws], axis=0)
        qs = jnp.concatenate([gather(Q_LO, q_rows), gather(Q_HI, q_rows)], axis=0)
        n_k = sum(r.stop - r.start for r in k_rows)
        s = _dot_nt(qs, gather(K_ALL, k_rows)) + mask[n_k]
        m = jnp.max(s, axis=-1, keepdims=True)
        p = jnp.exp2(s - m).astype(BF16)
        p = jnp.concatenate([p[:BAND], p[BAND:]], axis=1)
        reps = n_k // BAND
        rhs = jnp.concatenate(
            [jnp.concatenate([gather(V_LO, k_rows), jnp.concatenate([ones_lo] * reps, axis=0)], axis=1),
             jnp.concatenate([gather(V_HI, k_rows), jnp.concatenate([ones_hi] * reps, axis=0)], axis=1)],
            axis=0)
        r = _dot(p, rhs)
        return r[:, :LANES], r[:, LANES:], jnp.where(lo, m[:BAND], m[BAND:])

    def put(st, dst_rows, state, src_rows=slice(None)):
        for i in range(3):
            st[i, dst_rows, :] = state[i][src_rows]

    get = lambda st, r: [st[i, r, :] for i in range(3)]

    for blk in range(n_blk):
        k_rows = [rows(0)] if blk == 0 else [rows((blk - 1) * BAND, 2 * BAND)]
        put(st1, rows(blk * BAND), block(nat, [rows(blk * BAND)], k_rows, masks1))

    for c16 in range(n_blk):
        put(st16, rows(c16 * BAND), block(cls, [rows(c16 * BAND)], [rows(c16 * BAND)], masks1))

    for r4 in range(4):
        pieces = lambda kb: [rows((r4 + 4 * c) * BAND + kb * SUB, SUB) for c in range(4)]
        for kb in range(BAND // SUB):
            k_rows = pieces(kb) if kb == 0 else pieces(kb - 1) + pieces(kb)
            state = block(cls, pieces(kb), k_rows, masks4)
            for c, piece in enumerate(pieces(kb)):
                put(st4, piece, state, rows(c * SUB, SUB))

    for c16 in range(n_blk):
        r4, c = c16 % 4, c16 // 4
        put(tmp, pl.ds(r4 * quarter + c, BAND, stride=4),
            _merge(get(st4, rows(c16 * BAND)), get(st16, rows(c16 * BAND))))
    for r4 in range(4):
        for kb in range(quarter // BAND):
            put(stn, pl.ds(r4 + 4 * BAND * kb, BAND, stride=4), get(tmp, rows(r4 * quarter + kb * BAND)))
    for blk in range(n_blk):
        r = rows(blk * BAND)
        acc, den, _ = _merge(get(st1, r), get(stn, r))
        y_ref[r, :] = acc / den


def _attn(q, k, v, batch):
    t = q.shape[0]
    s_len = t // batch
    assert s_len == 16 * BAND and BAND == 4 * SUB
    slab = pl.BlockSpec((s_len, LANES), lambda b, j: (b, j))
    operands = pltpu.VMEM((5, s_len, LANES), BF16)
    state = pltpu.VMEM((3, s_len, LANES), F32)
    return pl.pallas_call(
        _attn_kernel,
        grid=(batch, D_ATT // LANES),
        in_specs=[slab, slab, slab],
        out_specs=slab,
        out_shape=jax.ShapeDtypeStruct((t, D_ATT), F32),
        scratch_shapes=[operands, operands] + [state] * 5,
        compiler_params=_params(2),
        name="dilated_attn",
    )(q, k, v)


def _mem_kv_kernel(m_ref, g_ref, wkt_ref, wv_ref, kt_ref, v_ref):
    mn = _rms(m_ref[...], g_ref[...]).astype(BF16)
    kt_ref[...] = (_dot_nt(wkt_ref[...], mn) * (X_HEAD_DIM ** -0.5)).astype(BF16)
    v_ref[...] = _dot(mn, wv_ref[...]).astype(BF16)


def _mem_kv(mem, g, wk_t, wv):
    b = mem.shape[0]
    return pl.pallas_call(
        _mem_kv_kernel,
        grid=(b,),
        in_specs=[pl.BlockSpec((None, N_MEM, D_MODEL), lambda i: (i, 0, 0)),
                  _const_spec((1, D_MODEL)), _const_spec((D_MODEL, D_MODEL)),
                  _const_spec((D_MODEL, D_MODEL))],
        out_specs=[pl.BlockSpec((None, D_MODEL, N_MEM), lambda i: (i, 0, 0)),
                   pl.BlockSpec((None, N_MEM, D_MODEL), lambda i: (i, 0, 0))],
        out_shape=[jax.ShapeDtypeStruct((b, D_MODEL, N_MEM), BF16),
                   jax.ShapeDtypeStruct((b, N_MEM, D_MODEL), BF16)],
        compiler_params=_params(1),
        name="mem_kv",
    )(mem, g, wk_t, wv)


def _cross_kernel(x_ref, ya_ref, yb_ref, nb_ref, wmix_ref, g_ref, wq_ref, kt_ref, v_ref, wo_ref, out_ref):
    yb = _rms(yb_ref[...], nb_ref[...]).astype(BF16)
    x = (x_ref[...] + _dot(ya_ref[...], wmix_ref[0:D_SGU, :])
         + _dot(yb, wmix_ref[D_SGU:D_SGU + D_ATT, :]))
    h = _rms(x, g_ref[...]).astype(BF16)
    q = _dot(h, wq_ref[...]).astype(BF16)
    heads = []
    for hh in range(X_HEADS):
        cols = slice(hh * X_HEAD_DIM, (hh + 1) * X_HEAD_DIM)
        s = _dot(q[:, cols], kt_ref[cols, :])
        p = jnp.exp(s - jnp.max(s, axis=-1, keepdims=True))
        l = jnp.sum(p, axis=-1, keepdims=True)
        heads.append((_dot(p.astype(BF16), v_ref[:, cols]) / l).astype(BF16))
    o = jnp.concatenate(heads, axis=1)
    out_ref[...] = x + _dot(o, wo_ref[...])


def _cross(x, ya, yb, norm_b, w_mix, g, wq, kt, v, wo, batch):
    t = x.shape[0]
    per_batch = t // batch // TOKEN_TILE
    full = pl.BlockSpec((TOKEN_TILE, D_MODEL), lambda b, i: (b * per_batch + i, 0))
    half = pl.BlockSpec((TOKEN_TILE, D_ATT), lambda b, i: (b * per_batch + i, 0))
    return pl.pallas_call(
        _cross_kernel,
        grid=(batch, per_batch),
        in_specs=[full, half, half, _const_spec((1, D_ATT)), _const_spec((D_MODEL, D_MODEL)),
                  _const_spec((1, D_MODEL)), _const_spec((D_MODEL, D_MODEL)),
                  pl.BlockSpec((None, D_MODEL, N_MEM), lambda b, i: (b, 0, 0)),
                  pl.BlockSpec((None, N_MEM, D_MODEL), lambda b, i: (b, 0, 0)),
                  _const_spec((D_MODEL, D_MODEL))],
        out_specs=full,
        out_shape=jax.ShapeDtypeStruct((t, D_MODEL), F32),
        compiler_params=_params(2),
        name="mix_out_cross",
    )(x, ya, yb, norm_b, w_mix, g, wq, kt, v, wo)


def _rope_lane_freq():
    half = ROPE_DIM // 2
    inv_freq = ROPE_THETA ** (-2.0 * jnp.arange(half, dtype=F32) / ROPE_DIM)
    e = jnp.arange(LANES) % HEAD_DIM
    return jnp.where(e < ROPE_DIM, inv_freq[e % half], 0.0).astype(F32).reshape(1, LANES)


def _group_mean_matrix():
    g = jnp.arange(D_SGU) // SGU_GROUP_DIM
    return jnp.where(g[:, None] == g[None, :], 1.0 / SGU_GROUP_DIM, 0.0).astype(BF16)


def kernel(x, mem, positions, ffn1_norm, ffn1_w_gate, ffn1_w_up, ffn1_w_down, mix_norm, w_in, sgu_ln_g, sgu_ln_b, sgu_w_s, sgu_b_s, out_norm_a, out_norm_b, w_out, cross_norm, mem_norm, cross_wq, cross_wk, cross_wv, cross_wo, ffn2_norm, ffn2_w_gate, ffn2_w_up, ffn2_w_down, final_norm):
    b, s, d = x.shape
    t = b * s
    depth = ffn1_norm.shape[0]
    row = lambda p: p.reshape(1, -1).astype(F32)
    xt = x.reshape(t, d)
    pos = positions.reshape(t, 1)
    freq = _rope_lane_freq()
    gmat = _group_mean_matrix()
    fin = row(final_norm)
    for l in range(depth):
        xt = _ffn(xt, row(ffn1_norm[l]), ffn1_w_gate[l].astype(BF16), ffn1_w_up[l].astype(BF16),
                  ffn1_w_down[l].astype(BF16), fin, final_norm=False)

        ya, q, k, v = _in_proj(xt, pos, row(mix_norm[l]), w_in[l].astype(BF16), row(sgu_ln_g[l]),
                               row(sgu_ln_b[l]), sgu_w_s[l], sgu_b_s[l].T, gmat, freq,
                               row(out_norm_a[l]))
        yb = _attn(q, k, v, b)
        kt, vm = _mem_kv(mem, row(mem_norm[l]), cross_wk[l].T.astype(BF16), cross_wv[l].astype(BF16))
        xt = _cross(xt, ya, yb, row(out_norm_b[l]), w_out[l].astype(BF16), row(cross_norm[l]),
                    cross_wq[l].astype(BF16), kt, vm, cross_wo[l].astype(BF16), b)

        xt = _ffn(xt, row(ffn2_norm[l]), ffn2_w_gate[l].astype(BF16), ffn2_w_up[l].astype(BF16),
                  ffn2_w_down[l].astype(BF16), fin, final_norm=(l == depth - 1))
    return xt.reshape(b, s, d)
```

```python
import functools

import jax
import jax.numpy as jnp
from jax import lax
from jax.experimental import pallas as pl
from jax.experimental.pallas import tpu as pltpu

F32 = jnp.float32
BF16 = jnp.bfloat16

D_MODEL = 1024
N_MEM = 256
D_SGU = 512
D_ATT = 512
SGU_GROUPS = 8
SGU_GROUP_DIM = D_SGU // SGU_GROUPS
CHUNK = 128
N_HEADS = 8
HEAD_DIM = D_ATT // N_HEADS
BAND = 128
ROPE_THETA = 500000.0
ROPE_DIM = HEAD_DIM // 4
X_HEADS = 4
X_HEAD_DIM = D_MODEL // X_HEADS
D_FF = 2816
NORM_EPS = 1e-6
LN_EPS = 1e-5
LOG2_E = 1.4426950408889634

LANES = 128
TOKEN_TILE = 512
FF_CHUNK = 512
W_STAGE_ROWS = 256
VMEM_LIMIT_BYTES = 56 * 1024 * 1024


def _rms(x, g):
    return x * lax.rsqrt(jnp.mean(x * x, axis=-1, keepdims=True) + NORM_EPS) * g


def _dot(a, b):
    return jnp.dot(a, b, preferred_element_type=F32)


def _dot_nt(a, b):
    return lax.dot_general(a, b, (((1,), (1,)), ((), ())), preferred_element_type=F32)


def _const_spec(shape):
    zeros = (0,) * len(shape)
    return pl.BlockSpec(shape, lambda *_: zeros, pipeline_mode=pl.Buffered(1))


def _stream_weight(w_hbm, w_vmem, stage, sem):
    rows, cols = w_vmem.shape
    chunk = stage.shape[1]
    assert rows % chunk == 0

    def copy(c):
        return pltpu.make_async_copy(w_hbm.at[pl.ds(c * chunk, chunk), :],
                                     stage.at[c % 2, :, pl.ds(0, cols)], sem.at[c % 2])

    copy(0).start()
    for c in range(rows // chunk):
        if c + 1 < rows // chunk:
            copy(c + 1).start()
        copy(c).wait()
        w_vmem[c * chunk:(c + 1) * chunk, :] = stage[c % 2, :, 0:cols].astype(BF16)


def _params(n_axes, **extra):
    return pltpu.CompilerParams(
        dimension_semantics=("arbitrary",) * n_axes,
        vmem_limit_bytes=VMEM_LIMIT_BYTES,
        **extra,
    )


FF_CHUNKS = tuple((c0, min(FF_CHUNK, D_FF - c0)) for c0 in range(0, D_FF, FF_CHUNK))


def _ffn_kernel(x_ref, g_ref, wg_hbm, wu_hbm, wd_hbm, fg_ref, o_ref,
                wg, wu, wd, stage_g, stage_u, stage_d, sem, *, final_norm):
    def chunk_copies(c, slot):
        c0, ck = FF_CHUNKS[c]
        return (
            pltpu.make_async_copy(wg_hbm.at[:, pl.ds(c0, ck)], stage_g.at[slot, :, pl.ds(0, ck)], sem.at[slot, 0]),
            pltpu.make_async_copy(wu_hbm.at[:, pl.ds(c0, ck)], stage_u.at[slot, :, pl.ds(0, ck)], sem.at[slot, 1]),
            pltpu.make_async_copy(wd_hbm.at[pl.ds(c0, ck), :], stage_d.at[slot, pl.ds(0, ck), :], sem.at[slot, 2]),
        )

    def run(load_weights):
        if load_weights:
            for cp in chunk_copies(0, 0):
                cp.start()
        x = x_ref[...]
        h = _rms(x, g_ref[...]).astype(BF16)
        acc = jnp.zeros(x.shape, F32)
        for c, (c0, ck) in enumerate(FF_CHUNKS):
            if load_weights:
                slot = c % 2
                if c + 1 < len(FF_CHUNKS):
                    for cp in chunk_copies(c + 1, 1 - slot):
                        cp.start()
                for cp in chunk_copies(c, slot):
                    cp.wait()
                wg[:, c0:c0 + ck] = stage_g[slot, :, 0:ck].astype(BF16)
                wu[:, c0:c0 + ck] = stage_u[slot, :, 0:ck].astype(BF16)
                wd[c0:c0 + ck, :] = stage_d[slot, 0:ck, :].astype(BF16)
            g = _dot(h, wg[:, c0:c0 + ck])
            u = _dot(h, wu[:, c0:c0 + ck])
            a = (g * jax.nn.sigmoid(g)) * u
            acc = acc + _dot(a.astype(BF16), wd[c0:c0 + ck, :])
        y = x + 0.5 * acc
        if final_norm:
            y = _rms(y, fg_ref[...])
        o_ref[...] = y

    first = pl.program_id(0) == 0
    pl.when(first)(functools.partial(run, True))
    pl.when(jnp.logical_not(first))(functools.partial(run, False))


def _ffn(x, g, wg, wu, wd, fg, *, final_norm):
    t = x.shape[0]
    tile = pl.BlockSpec((TOKEN_TILE, D_MODEL), lambda i: (i, 0))
    hbm = pl.BlockSpec(memory_space=pl.ANY)
    return pl.pallas_call(
        functools.partial(_ffn_kernel, final_norm=final_norm),
        grid=(t // TOKEN_TILE,),
        in_specs=[tile, _const_spec((1, D_MODEL)), hbm, hbm, hbm, _const_spec((1, D_MODEL))],
        out_specs=tile,
        out_shape=jax.ShapeDtypeStruct((t, D_MODEL), F32),
        scratch_shapes=[pltpu.VMEM((D_MODEL, D_FF), BF16), pltpu.VMEM((D_MODEL, D_FF), BF16),
                        pltpu.VMEM((D_FF, D_MODEL), BF16),
                        pltpu.VMEM((2, D_MODEL, FF_CHUNK), F32), pltpu.VMEM((2, D_MODEL, FF_CHUNK), F32),
                        pltpu.VMEM((2, FF_CHUNK, D_MODEL), F32),
                        pltpu.SemaphoreType.DMA((2, 3))],
        compiler_params=_params(1),
        name="ffn_final" if final_norm else "ffn",
    )(x, g, wg, wu, wd, fg)


def _gelu(x):
    return 0.5 * x * (1.0 + lax.erf(x * (2.0 ** -0.5)))


def _group_mean(x, gmat):
    return _dot(x.astype(BF16), gmat)


def _rotary(t, cos, sin_signed, first_half):
    parts = []
    for j in range(D_ATT // LANES):
        tj = t[:, j * LANES:(j + 1) * LANES]
        partner = jnp.where(first_half,
                            pltpu.roll(tj, LANES - ROPE_DIM // 2, axis=1),
                            pltpu.roll(tj, ROPE_DIM // 2, axis=1))
        parts.append(tj * cos + partner * sin_signed)
    return jnp.concatenate(parts, axis=1)


def _in_proj_kernel(x_ref, pos_ref, g_ref, win_hbm, lng_ref, lnb_ref, ws_ref, bst_ref,
                    gmat_ref, freq_ref, na_ref, ya_ref, q_ref, k_ref, v_ref,
                    ya_scr, win_ref, stage, sem):
    @pl.when(pl.program_id(0) == 0)
    def _():
        _stream_weight(win_hbm, win_ref, stage, sem)

    tm = x_ref.shape[0]
    n_chunks = tm // CHUNK
    h = _rms(x_ref[...], g_ref[...]).astype(BF16)

    def proj(c0, width):
        return _dot(h, win_ref[:, c0:c0 + width])

    u = _gelu(proj(0, D_SGU))
    v = _gelu(proj(D_SGU, D_SGU))
    gmat = gmat_ref[...]
    d = v - _group_mean(v, gmat)
    var = _group_mean(d * d, gmat)
    vln = d * lax.rsqrt(var + LN_EPS) * lng_ref[...] + lnb_ref[...]

    lane = lax.broadcasted_iota(jnp.int32, (CHUNK, LANES), 1)
    low_group = lane < SGU_GROUP_DIM
    row = lax.broadcasted_iota(jnp.int32, (CHUNK, CHUNK), 0)
    col = lax.broadcasted_iota(jnp.int32, (CHUNK, CHUNK), 1)
    causal = col <= row
    bst = bst_ref[...]
    for j in range(D_SGU // LANES):
        g0, g1 = 2 * j, 2 * j + 1
        w0 = jnp.where(causal, ws_ref[g0], 0.0)
        w1 = jnp.where(causal, ws_ref[g1], 0.0)
        wcat = jnp.concatenate([w0, w1], axis=1).astype(BF16)
        vj = vln[:, j * LANES:(j + 1) * LANES]
        top = jnp.concatenate(
            [jnp.where(low_group, vj[c * CHUNK:(c + 1) * CHUNK], 0.0) for c in range(n_chunks)], axis=1)
        bot = jnp.concatenate(
            [jnp.where(low_group, 0.0, vj[c * CHUNK:(c + 1) * CHUNK]) for c in range(n_chunks)], axis=1)
        rhs = jnp.concatenate([top, bot], axis=0).astype(BF16)
        mixed = _dot(wcat, rhs)
        bias = jnp.where(low_group,
                         jnp.broadcast_to(bst[:, g0:g0 + 1], (CHUNK, LANES)),
                         jnp.broadcast_to(bst[:, g1:g1 + 1], (CHUNK, LANES)))
        for c in range(n_chunks):
            uj = u[c * CHUNK:(c + 1) * CHUNK, j * LANES:(j + 1) * LANES]
            ya_scr[c * CHUNK:(c + 1) * CHUNK, j * LANES:(j + 1) * LANES] = (
                uj * (mixed[:, c * LANES:(c + 1) * LANES] + bias))
    ya_ref[...] = _rms(ya_scr[...], na_ref[...]).astype(BF16)

    ang = pos_ref[...].astype(F32) * freq_ref[...]
    cos = jnp.cos(ang)
    sin = jnp.sin(ang)
    lane_t = lax.broadcasted_iota(jnp.int32, (tm, LANES), 1)
    first_half = (lane_t & (HEAD_DIM - 1)) < ROPE_DIM // 2
    sin_signed = jnp.where(first_half, -sin, sin)
    q = proj(2 * D_SGU, D_ATT) * (HEAD_DIM ** -0.5 * LOG2_E)
    q_ref[...] = _rotary(q, cos, sin_signed, first_half)
    k = proj(2 * D_SGU + D_ATT, D_ATT)
    k_ref[...] = _rotary(k, cos, sin_signed, first_half)
    v_ref[...] = proj(2 * D_SGU + 2 * D_ATT, D_ATT)


def _in_proj(x, pos, g, w_in, ln_g, ln_b, w_s, b_st, gmat, freq, norm_a):
    t = x.shape[0]
    d_in = w_in.shape[1]
    half = pl.BlockSpec((TOKEN_TILE, D_SGU), lambda i: (i, 0))
    return pl.pallas_call(
        _in_proj_kernel,
        grid=(t // TOKEN_TILE,),
        in_specs=[pl.BlockSpec((TOKEN_TILE, D_MODEL), lambda i: (i, 0)),
                  pl.BlockSpec((TOKEN_TILE, 1), lambda i: (i, 0)),
                  _const_spec((1, D_MODEL)), pl.BlockSpec(memory_space=pl.ANY),
                  _const_spec((1, D_SGU)), _const_spec((1, D_SGU)),
                  _const_spec((SGU_GROUPS, CHUNK, CHUNK)), _const_spec((CHUNK, SGU_GROUPS)),
                  _const_spec((D_SGU, D_SGU)), _const_spec((1, LANES)), _const_spec((1, D_SGU))],
        out_specs=[half, half, half, half],
        out_shape=[jax.ShapeDtypeStruct((t, D_SGU), BF16)] + [jax.ShapeDtypeStruct((t, D_ATT), F32)] * 3,
        scratch_shapes=[pltpu.VMEM((TOKEN_TILE, D_SGU), F32), pltpu.VMEM((D_MODEL, d_in), BF16),
                        pltpu.VMEM((2, W_STAGE_ROWS, d_in), F32), pltpu.SemaphoreType.DMA((2,))],
        compiler_params=_params(1),
        name="in_proj",
    )(x, pos, g, w_in, ln_g, ln_b, w_s, b_st, gmat, freq, norm_a)


Q_LO, Q_HI, K_ALL, V_LO, V_HI = range(5)
ACC, DEN, MAX = range(3)
SUB = BAND // 4


def _merge(a, b):
    m = jnp.maximum(a[MAX], b[MAX])
    w_a = jnp.exp2(a[MAX] - m)
    w_b = jnp.exp2(b[MAX] - m)
    return w_a * a[ACC] + w_b * b[ACC], w_a * a[DEN] + w_b * b[DEN], m


def _attn_kernel(q_ref, k_ref, v_ref, y_ref, nat, cls, tmp, st1, st4, st16, stn):
    s_len = q_ref.shape[0]
    n_blk = s_len // BAND
    quarter = s_len // 4
    rows = lambda start, n=BAND: slice(start, start + n)
    lane = lax.broadcasted_iota(jnp.int32, (BAND, LANES), 1)
    lo = lane < HEAD_DIM
    ones_lo = jnp.where(lo, 1.0, 0.0).astype(BF16)
    ones_hi = jnp.where(lo, 0.0, 1.0).astype(BF16)

    def fill(dst, blk, qkv):
        q, k, v = qkv
        dst_rows = rows(blk * BAND)
        dst[Q_LO, dst_rows, :] = jnp.where(lo, q, 0.0).astype(BF16)
        dst[Q_HI, dst_rows, :] = jnp.where(lo, 0.0, q).astype(BF16)
        dst[K_ALL, dst_rows, :] = k.astype(BF16)
        dst[V_LO, dst_rows, :] = jnp.where(lo, v, 0.0).astype(BF16)
        dst[V_HI, dst_rows, :] = jnp.where(lo, 0.0, v).astype(BF16)

    srcs = (q_ref, k_ref, v_ref)
    for r4 in range(4):
        for kb in range(quarter // BAND):
            for i, src in enumerate(srcs):
                tmp[i, rows(r4 * quarter + kb * BAND), :] = src[pl.ds(r4 + 4 * BAND * kb, BAND, stride=4), :]
    for blk in range(n_blk):
        fill(nat, blk, [src[rows(blk * BAND), :] for src in srcs])
    for c16 in range(n_blk):
        r4, c = c16 % 4, c16 // 4
        fill(cls, c16, [tmp[i, pl.ds(r4 * quarter + c, BAND, stride=4), :] for i in range(3)])

    qi = lax.broadcasted_iota(jnp.int32, (2 * BAND, BAND), 0) & (BAND - 1)
    kj = lax.broadcasted_iota(jnp.int32, (2 * BAND, BAND), 1)

    def masks(pos):
        neg = jnp.float32(-jnp.inf)
        prev = jnp.where(pos(kj) >= pos(qi), 0.0, neg)
        cur = jnp.where(pos(kj) <= pos(qi), 0.0, neg)
        return {BAND: cur, 2 * BAND: jnp.concatenate([prev, cur], axis=1)}

    masks1 = masks(lambda i: i)
    sub_shift = SUB.bit_length() - 1
    masks4 = masks(lambda i: 4 * (i & (SUB - 1)) + (i >> sub_shift))

    def block(src, q_rows, k_rows, mask):
        gather = lambda idx, rws: jnp.concatenate([src[idx, r, :] for r in rws], axis=0)
        qs = jnp.concatenate([gather(Q_LO, q_rows), gather(Q_HI, q_rows)], axis=0)
        n_k = sum(r.stop - r.start for r in k_rows)
        s = _dot_nt(qs, gather(K_ALL, k_rows)) + mask[n_k]
        m = jnp.max(s, axis=-1, keepdims=True)
        p = jnp.exp2(s - m).astype(BF16)
        p = jnp.concatenate([p[:BAND], p[BAND:]], axis=1)
        reps = n_k // BAND
        rhs = jnp.concatenate(
            [jnp.concatenate([gather(V_LO, k_rows), jnp.concatenate([ones_lo] * reps, axis=0)], axis=1),
             jnp.concatenate([gather(V_HI, k_rows), jnp.concatenate([ones_hi] * reps, axis=0)], axis=1)],
            axis=0)
        r = _dot(p, rhs)
        return r[:, :LANES], r[:, LANES:], jnp.where(lo, m[:BAND], m[BAND:])

    def put(st, dst_rows, state, src_rows=slice(None)):
        for i in range(3):
            st[i, dst_rows, :] = state[i][src_rows]

    get = lambda st, r: [st[i, r, :] for i in range(3)]

    for blk in range(n_blk):
        k_rows = [rows(0)] if blk == 0 else [rows((blk - 1) * BAND, 2 * BAND)]
        put(st1, rows(blk * BAND), block(nat, [rows(blk * BAND)], k_rows, masks1))

    for c16 in range(n_blk):
        put(st16, rows(c16 * BAND), block(cls, [rows(c16 * BAND)], [rows(c16 * BAND)], masks1))

    for r4 in range(4):
        pieces = lambda kb: [rows((r4 + 4 * c) * BAND + kb * SUB, SUB) for c in range(4)]
        for kb in range(BAND // SUB):
            k_rows = pieces(kb) if kb == 0 else pieces(kb - 1) + pieces(kb)
            state = block(cls, pieces(kb), k_rows, masks4)
            for c, piece in enumerate(pieces(kb)):
                put(st4, piece, state, rows(c * SUB, SUB))

    for c16 in range(n_blk):
        r4, c = c16 % 4, c16 // 4
        put(tmp, pl.ds(r4 * quarter + c, BAND, stride=4),
            _merge(get(st4, rows(c16 * BAND)), get(st16, rows(c16 * BAND))))
    for r4 in range(4):
        for kb in range(quarter // BAND):
            put(stn, pl.ds(r4 + 4 * BAND * kb, BAND, stride=4), get(tmp, rows(r4 * quarter + kb * BAND)))
    for blk in range(n_blk):
        r = rows(blk * BAND)
        acc, den, _ = _merge(get(st1, r), get(stn, r))
        y_ref[r, :] = acc / den


def _attn(q, k, v, batch):
    t = q.shape[0]
    s_len = t // batch
    assert s_len == 16 * BAND and BAND == 4 * SUB
    slab = pl.BlockSpec((s_len, LANES), lambda b, j: (b, j))
    operands = pltpu.VMEM((5, s_len, LANES), BF16)
    state = pltpu.VMEM((3, s_len, LANES), F32)
    return pl.pallas_call(
        _attn_kernel,
        grid=(batch, D_ATT // LANES),
        in_specs=[slab, slab, slab],
        out_specs=slab,
        out_shape=jax.ShapeDtypeStruct((t, D_ATT), F32),
        scratch_shapes=[operands, operands] + [state] * 5,
        compiler_params=_params(2),
        name="dilated_attn",
    )(q, k, v)


def _mem_kv_kernel(m_ref, g_ref, wk_hbm, wv_hbm, k_ref, v_ref, wk_ref, wv_ref, stage, sem):
    @pl.when(pl.program_id(0) == 0)
    def _():
        _stream_weight(wk_hbm, wk_ref, stage, sem)
        _stream_weight(wv_hbm, wv_ref, stage, sem)

    mn = _rms(m_ref[...], g_ref[...]).astype(BF16)
    k_ref[...] = (_dot(mn, wk_ref[...]) * (X_HEAD_DIM ** -0.5)).astype(BF16)
    v_ref[...] = _dot(mn, wv_ref[...]).astype(BF16)


def _mem_kv(mem, g, wk, wv):
    b = mem.shape[0]
    hbm = pl.BlockSpec(memory_space=pl.ANY)
    per_batch = pl.BlockSpec((None, N_MEM, D_MODEL), lambda i: (i, 0, 0))
    weight = pltpu.VMEM((D_MODEL, D_MODEL), BF16)
    return pl.pallas_call(
        _mem_kv_kernel,
        grid=(b,),
        in_specs=[per_batch, _const_spec((1, D_MODEL)), hbm, hbm],
        out_specs=[per_batch, per_batch],
        out_shape=[jax.ShapeDtypeStruct((b, N_MEM, D_MODEL), BF16)] * 2,
        scratch_shapes=[weight, weight, pltpu.VMEM((2, W_STAGE_ROWS, D_MODEL), F32),
                        pltpu.SemaphoreType.DMA((2,))],
        compiler_params=_params(1),
        name="mem_kv",
    )(mem, g, wk, wv)


def _cross_kernel(x_ref, ya_ref, yb_ref, nb_ref, wmix_hbm, g_ref, wq_hbm, k_ref, v_ref, wo_hbm, out_ref,
                  wmix_ref, wq_ref, wo_ref, stage, sem):
    @pl.when((pl.program_id(0) == 0) & (pl.program_id(1) == 0))
    def _():
        _stream_weight(wmix_hbm, wmix_ref, stage, sem)
        _stream_weight(wq_hbm, wq_ref, stage, sem)
        _stream_weight(wo_hbm, wo_ref, stage, sem)

    yb = _rms(yb_ref[...], nb_ref[...]).astype(BF16)
    x = (x_ref[...] + _dot(ya_ref[...], wmix_ref[0:D_SGU, :])
         + _dot(yb, wmix_ref[D_SGU:D_SGU + D_ATT, :]))
    h = _rms(x, g_ref[...]).astype(BF16)
    q = _dot(h, wq_ref[...]).astype(BF16)
    heads = []
    for hh in range(X_HEADS):
        cols = slice(hh * X_HEAD_DIM, (hh + 1) * X_HEAD_DIM)
        s = _dot_nt(q[:, cols], k_ref[:, cols])
        p = jnp.exp(s - jnp.max(s, axis=-1, keepdims=True))
        l = jnp.sum(p, axis=-1, keepdims=True)
        heads.append((_dot(p.astype(BF16), v_ref[:, cols]) / l).astype(BF16))
    o = jnp.concatenate(heads, axis=1)
    out_ref[...] = x + _dot(o, wo_ref[...])


def _cross(x, ya, yb, norm_b, w_mix, g, wq, k, v, wo, batch):
    t = x.shape[0]
    per_batch = t // batch // TOKEN_TILE
    full = pl.BlockSpec((TOKEN_TILE, D_MODEL), lambda b, i: (b * per_batch + i, 0))
    half = pl.BlockSpec((TOKEN_TILE, D_ATT), lambda b, i: (b * per_batch + i, 0))
    mem_rows = pl.BlockSpec((None, N_MEM, D_MODEL), lambda b, i: (b, 0, 0))
    hbm = pl.BlockSpec(memory_space=pl.ANY)
    weight = pltpu.VMEM((D_MODEL, D_MODEL), BF16)
    return pl.pallas_call(
        _cross_kernel,
        grid=(batch, per_batch),
        in_specs=[full, half, half, _const_spec((1, D_ATT)), hbm, _const_spec((1, D_MODEL)), hbm,
                  mem_rows, mem_rows, hbm],
        out_specs=full,
        out_shape=jax.ShapeDtypeStruct((t, D_MODEL), F32),
        scratch_shapes=[weight, weight, weight, pltpu.VMEM((2, W_STAGE_ROWS, D_MODEL), F32),
                        pltpu.SemaphoreType.DMA((2,))],
        compiler_params=_params(2),
        name="mix_out_cross",
    )(x, ya, yb, norm_b, w_mix, g, wq, k, v, wo)


def _rope_lane_freq():
    half = ROPE_DIM // 2
    inv_freq = ROPE_THETA ** (-2.0 * jnp.arange(half, dtype=F32) / ROPE_DIM)
    e = jnp.arange(LANES) % HEAD_DIM
    return jnp.where(e < ROPE_DIM, inv_freq[e % half], 0.0).astype(F32).reshape(1, LANES)


def _group_mean_matrix():
    g = jnp.arange(D_SGU) // SGU_GROUP_DIM
    return jnp.where(g[:, None] == g[None, :], 1.0 / SGU_GROUP_DIM, 0.0).astype(BF16)


def kernel(x, mem, positions, ffn1_norm, ffn1_w_gate, ffn1_w_up, ffn1_w_down, mix_norm, w_in, sgu_ln_g, sgu_ln_b, sgu_w_s, sgu_b_s, out_norm_a, out_norm_b, w_out, cross_norm, mem_norm, cross_wq, cross_wk, cross_wv, cross_wo, ffn2_norm, ffn2_w_gate, ffn2_w_up, ffn2_w_down, final_norm):
    b, s, d = x.shape
    t = b * s
    depth = ffn1_norm.shape[0]
    row = lambda p: p.reshape(1, -1).astype(F32)
    xt = x.reshape(t, d)
    pos = positions.reshape(t, 1)
    freq = _rope_lane_freq()
    gmat = _group_mean_matrix()
    fin = row(final_norm)
    for l in range(depth):
        xt = _ffn(xt, row(ffn1_norm[l]), ffn1_w_gate[l], ffn1_w_up[l], ffn1_w_down[l], fin,
                  final_norm=False)

        ya, q, k, v = _in_proj(xt, pos, row(mix_norm[l]), w_in[l], row(sgu_ln_g[l]),
                               row(sgu_ln_b[l]), sgu_w_s[l], sgu_b_s[l].T, gmat, freq,
                               row(out_norm_a[l]))
        yb = _attn(q, k, v, b)
        km, vm = _mem_kv(mem, row(mem_norm[l]), cross_wk[l], cross_wv[l])
        xt = _cross(xt, ya, yb, row(out_norm_b[l]), w_out[l], row(cross_norm[l]),
                    cross_wq[l], km, vm, cross_wo[l], b)

        xt = _ffn(xt, row(ffn2_norm[l]), ffn2_w_gate[l], ffn2_w_up[l], ffn2_w_down[l], fin,
                  final_norm=(l == depth - 1))
    return xt.reshape(b, s, d)
```

```python
import functools

import numpy as np
import jax
import jax.numpy as jnp
from jax import lax
from jax.experimental import pallas as pl
from jax.experimental.pallas import tpu as pltpu

F32 = jnp.float32
BF16 = jnp.bfloat16

D_MODEL = 1024
N_MEM = 256
D_SGU = 512
D_ATT = 512
SGU_GROUPS = 8
SGU_GROUP_DIM = D_SGU // SGU_GROUPS
CHUNK = 128
N_HEADS = 8
HEAD_DIM = D_ATT // N_HEADS
BAND = 128
ROPE_THETA = 500000.0
ROPE_DIM = HEAD_DIM // 4
X_HEADS = 4
X_HEAD_DIM = D_MODEL // X_HEADS
D_FF = 2816
NORM_EPS = 1e-6
LN_EPS = 1e-5
LOG2_E = 1.4426950408889634

LANES = 128
TOKEN_TILE = 512
CROSS_TILE = 1024
FF_CHUNK = 512
W_STAGE_ROWS = 256
VMEM_LIMIT_BYTES = 56 * 1024 * 1024


def _rms(x, g):
    return x * lax.rsqrt(jnp.mean(x * x, axis=-1, keepdims=True) + NORM_EPS) * g


def _dot(a, b):
    return jnp.dot(a, b, preferred_element_type=F32)


def _dot_nt(a, b):
    return lax.dot_general(a, b, (((1,), (1,)), ((), ())), preferred_element_type=F32)


def _const_spec(shape):
    zeros = (0,) * len(shape)
    return pl.BlockSpec(shape, lambda *_: zeros, pipeline_mode=pl.Buffered(1))


def _stream_weight(w_hbm, w_vmem, stage, sem):
    rows, cols = w_vmem.shape
    chunk = stage.shape[1]
    assert rows % chunk == 0

    def copy(c):
        return pltpu.make_async_copy(w_hbm.at[pl.ds(c * chunk, chunk), :],
                                     stage.at[c % 2, :, pl.ds(0, cols)], sem.at[c % 2])

    copy(0).start()
    for c in range(rows // chunk):
        if c + 1 < rows // chunk:
            copy(c + 1).start()
        copy(c).wait()
        w_vmem[c * chunk:(c + 1) * chunk, :] = stage[c % 2, :, 0:cols].astype(BF16)


def _params(n_axes, **extra):
    return pltpu.CompilerParams(
        dimension_semantics=("arbitrary",) * n_axes,
        vmem_limit_bytes=VMEM_LIMIT_BYTES,
        **extra,
    )


FF_CHUNKS = tuple((c0, min(FF_CHUNK, D_FF - c0)) for c0 in range(0, D_FF, FF_CHUNK))


def _ffn_kernel(x_ref, g_ref, wg_hbm, wu_hbm, wd_hbm, fg_ref, o_ref,
                wg, wu, wd, stage_g, stage_u, stage_d, sem, *, final_norm):
    def chunk_copies(c, slot):
        c0, ck = FF_CHUNKS[c]
        return (
            pltpu.make_async_copy(wg_hbm.at[:, pl.ds(c0, ck)], stage_g.at[slot, :, pl.ds(0, ck)], sem.at[slot, 0]),
            pltpu.make_async_copy(wu_hbm.at[:, pl.ds(c0, ck)], stage_u.at[slot, :, pl.ds(0, ck)], sem.at[slot, 1]),
            pltpu.make_async_copy(wd_hbm.at[pl.ds(c0, ck), :], stage_d.at[slot, pl.ds(0, ck), :], sem.at[slot, 2]),
        )

    def run(load_weights):
        if load_weights:
            for cp in chunk_copies(0, 0):
                cp.start()
        x = x_ref[...]
        h = _rms(x, g_ref[...]).astype(BF16)
        acc = jnp.zeros(x.shape, F32)
        for c, (c0, ck) in enumerate(FF_CHUNKS):
            if load_weights:
                slot = c % 2
                if c + 1 < len(FF_CHUNKS):
                    for cp in chunk_copies(c + 1, 1 - slot):
                        cp.start()
                for cp in chunk_copies(c, slot):
                    cp.wait()
                wg[:, c0:c0 + ck] = stage_g[slot, :, 0:ck].astype(BF16)
                wu[:, c0:c0 + ck] = stage_u[slot, :, 0:ck].astype(BF16)
                wd[c0:c0 + ck, :] = stage_d[slot, 0:ck, :].astype(BF16)
            g = _dot(h, wg[:, c0:c0 + ck])
            u = _dot(h, wu[:, c0:c0 + ck])
            a = (g * jax.nn.sigmoid(g)) * u
            acc = acc + _dot(a.astype(BF16), wd[c0:c0 + ck, :])
        y = x + 0.5 * acc
        if final_norm:
            y = _rms(y, fg_ref[...])
        o_ref[...] = y

    first = pl.program_id(0) == 0
    pl.when(first)(functools.partial(run, True))
    pl.when(jnp.logical_not(first))(functools.partial(run, False))


def _ffn(x, g, wg, wu, wd, fg, *, final_norm):
    t = x.shape[0]
    tile = pl.BlockSpec((TOKEN_TILE, D_MODEL), lambda i: (i, 0))
    hbm = pl.BlockSpec(memory_space=pl.ANY)
    return pl.pallas_call(
        functools.partial(_ffn_kernel, final_norm=final_norm),
        grid=(t // TOKEN_TILE,),
        in_specs=[tile, _const_spec((1, D_MODEL)), hbm, hbm, hbm, _const_spec((1, D_MODEL))],
        out_specs=tile,
        out_shape=jax.ShapeDtypeStruct((t, D_MODEL), F32),
        scratch_shapes=[pltpu.VMEM((D_MODEL, D_FF), BF16), pltpu.VMEM((D_MODEL, D_FF), BF16),
                        pltpu.VMEM((D_FF, D_MODEL), BF16),
                        pltpu.VMEM((2, D_MODEL, FF_CHUNK), F32), pltpu.VMEM((2, D_MODEL, FF_CHUNK), F32),
                        pltpu.VMEM((2, FF_CHUNK, D_MODEL), F32),
                        pltpu.SemaphoreType.DMA((2, 3))],
        compiler_params=_params(1),
        name="ffn_final" if final_norm else "ffn",
    )(x, g, wg, wu, wd, fg)


def _gelu(x):
    return 0.5 * x * (1.0 + lax.erf(x * (2.0 ** -0.5)))


def _group_mean(x, gmat):
    return _dot(x.astype(BF16), gmat)


def _rotary(t, cos, sin_signed, first_half):
    parts = []
    for j in range(D_ATT // LANES):
        tj = t[:, j * LANES:(j + 1) * LANES]
        partner = jnp.where(first_half,
                            pltpu.roll(tj, LANES - ROPE_DIM // 2, axis=1),
                            pltpu.roll(tj, ROPE_DIM // 2, axis=1))
        parts.append(tj * cos + partner * sin_signed)
    return jnp.concatenate(parts, axis=1)


def _in_proj_kernel(x_ref, pos_ref, g_ref, win_hbm, lng_ref, lnb_ref, ws_ref, bst_ref,
                    gmat_ref, freq_ref, na_ref, ya_ref, q_ref, k_ref, v_ref,
                    ya_scr, win_ref, stage, sem):
    @pl.when(pl.program_id(0) == 0)
    def _():
        _stream_weight(win_hbm, win_ref, stage, sem)

    tm = x_ref.shape[0]
    n_chunks = tm // CHUNK
    h = _rms(x_ref[...], g_ref[...]).astype(BF16)

    def proj(c0, width):
        return _dot(h, win_ref[:, c0:c0 + width])

    u = _gelu(proj(0, D_SGU))
    v = _gelu(proj(D_SGU, D_SGU))
    gmat = gmat_ref[...]
    d = v - _group_mean(v, gmat)
    var = _group_mean(d * d, gmat)
    vln = d * lax.rsqrt(var + LN_EPS) * lng_ref[...] + lnb_ref[...]

    lane = lax.broadcasted_iota(jnp.int32, (CHUNK, LANES), 1)
    low_group = lane < SGU_GROUP_DIM
    row = lax.broadcasted_iota(jnp.int32, (CHUNK, CHUNK), 0)
    col = lax.broadcasted_iota(jnp.int32, (CHUNK, CHUNK), 1)
    causal = col <= row
    bst = bst_ref[...]
    for j in range(D_SGU // LANES):
        g0, g1 = 2 * j, 2 * j + 1
        w0 = jnp.where(causal, ws_ref[g0], 0.0)
        w1 = jnp.where(causal, ws_ref[g1], 0.0)
        wcat = jnp.concatenate([w0, w1], axis=1).astype(BF16)
        vj = vln[:, j * LANES:(j + 1) * LANES]
        top = jnp.concatenate(
            [jnp.where(low_group, vj[c * CHUNK:(c + 1) * CHUNK], 0.0) for c in range(n_chunks)], axis=1)
        bot = jnp.concatenate(
            [jnp.where(low_group, 0.0, vj[c * CHUNK:(c + 1) * CHUNK]) for c in range(n_chunks)], axis=1)
        rhs = jnp.concatenate([top, bot], axis=0).astype(BF16)
        mixed = _dot(wcat, rhs)
        bias = jnp.where(low_group,
                         jnp.broadcast_to(bst[:, g0:g0 + 1], (CHUNK, LANES)),
                         jnp.broadcast_to(bst[:, g1:g1 + 1], (CHUNK, LANES)))
        for c in range(n_chunks):
            uj = u[c * CHUNK:(c + 1) * CHUNK, j * LANES:(j + 1) * LANES]
            ya_scr[c * CHUNK:(c + 1) * CHUNK, j * LANES:(j + 1) * LANES] = (
                uj * (mixed[:, c * LANES:(c + 1) * LANES] + bias))
    ya_ref[...] = _rms(ya_scr[...], na_ref[...]).astype(BF16)

    ang = pos_ref[...].astype(F32) * freq_ref[...]
    cos = jnp.cos(ang)
    sin = jnp.sin(ang)
    lane_t = lax.broadcasted_iota(jnp.int32, (tm, LANES), 1)
    first_half = (lane_t & (HEAD_DIM - 1)) < ROPE_DIM // 2
    sin_signed = jnp.where(first_half, -sin, sin)
    q = proj(2 * D_SGU, D_ATT) * (HEAD_DIM ** -0.5 * LOG2_E)
    q_ref[...] = _rotary(q, cos, sin_signed, first_half)
    k = proj(2 * D_SGU + D_ATT, D_ATT)
    k_ref[...] = _rotary(k, cos, sin_signed, first_half)
    v_ref[...] = proj(2 * D_SGU + 2 * D_ATT, D_ATT)


def _in_proj(x, pos, g, w_in, ln_g, ln_b, w_s, b_st, gmat, freq, norm_a):
    t = x.shape[0]
    d_in = w_in.shape[1]
    half = pl.BlockSpec((TOKEN_TILE, D_SGU), lambda i: (i, 0))
    return pl.pallas_call(
        _in_proj_kernel,
        grid=(t // TOKEN_TILE,),
        in_specs=[pl.BlockSpec((TOKEN_TILE, D_MODEL), lambda i: (i, 0)),
                  pl.BlockSpec((TOKEN_TILE, 1), lambda i: (i, 0)),
                  _const_spec((1, D_MODEL)), pl.BlockSpec(memory_space=pl.ANY),
                  _const_spec((1, D_SGU)), _const_spec((1, D_SGU)),
                  _const_spec((SGU_GROUPS, CHUNK, CHUNK)), _const_spec((CHUNK, SGU_GROUPS)),
                  _const_spec((D_SGU, D_SGU)), _const_spec((1, LANES)), _const_spec((1, D_SGU))],
        out_specs=[half, half, half, half],
        out_shape=[jax.ShapeDtypeStruct((t, D_SGU), BF16)] + [jax.ShapeDtypeStruct((t, D_ATT), F32)] * 3,
        scratch_shapes=[pltpu.VMEM((TOKEN_TILE, D_SGU), F32), pltpu.VMEM((D_MODEL, d_in), BF16),
                        pltpu.VMEM((2, W_STAGE_ROWS, d_in), F32), pltpu.SemaphoreType.DMA((2,))],
        compiler_params=_params(1),
        name="in_proj",
    )(x, pos, g, w_in, ln_g, ln_b, w_s, b_st, gmat, freq, norm_a)


Q_LO, Q_HI, K_ALL, V_LO, V_HI = range(5)
ACC, DEN, MAX = range(3)
SUB = BAND // 4


def _merge(a, b):
    m = jnp.maximum(a[MAX], b[MAX])
    w_a = jnp.exp2(a[MAX] - m)
    w_b = jnp.exp2(b[MAX] - m)
    return w_a * a[ACC] + w_b * b[ACC], w_a * a[DEN] + w_b * b[DEN], m


def _attn_kernel(q_ref, k_ref, v_ref, y_ref, c4, c16, tmp, st16):
    s_len = q_ref.shape[0]
    n_blk = s_len // BAND
    quarter = s_len // 4
    rows = lambda start, n=BAND: slice(start, start + n)
    lane = lax.broadcasted_iota(jnp.int32, (BAND, LANES), 1)
    lo = lane < HEAD_DIM
    ones_lo = jnp.where(lo, 1.0, 0.0).astype(BF16)
    ones_hi = jnp.where(lo, 0.0, 1.0).astype(BF16)

    def fill(dst, blk, qkv):
        q, k, v = qkv
        dst_rows = rows(blk * BAND)
        dst[Q_LO, dst_rows, :] = jnp.where(lo, q, 0.0).astype(BF16)
        dst[Q_HI, dst_rows, :] = jnp.where(lo, 0.0, q).astype(BF16)
        dst[K_ALL, dst_rows, :] = k.astype(BF16)
        dst[V_LO, dst_rows, :] = jnp.where(lo, v, 0.0).astype(BF16)
        dst[V_HI, dst_rows, :] = jnp.where(lo, 0.0, v).astype(BF16)

    srcs = (q_ref, k_ref, v_ref)
    for r4 in range(4):
        for kb in range(quarter // BAND):
            for i, src in enumerate(srcs):
                tmp[i, rows(r4 * quarter + kb * BAND), :] = src[pl.ds(r4 + 4 * BAND * kb, BAND, stride=4), :]
    for blk in range(n_blk):
        fill(c4, blk, [tmp[i, rows(blk * BAND), :] for i in range(3)])
    for cls in range(n_blk):
        r4, c = cls % 4, cls // 4
        fill(c16, cls, [tmp[i, pl.ds(r4 * quarter + c, BAND, stride=4), :] for i in range(3)])

    qi = lax.broadcasted_iota(jnp.int32, (2 * BAND, BAND), 0) & (BAND - 1)
    kj = lax.broadcasted_iota(jnp.int32, (2 * BAND, BAND), 1)

    def masks(pos):
        neg = jnp.float32(-jnp.inf)
        prev = jnp.where(pos(kj) >= pos(qi), 0.0, neg)
        cur = jnp.where(pos(kj) <= pos(qi), 0.0, neg)
        return {BAND: cur, 2 * BAND: jnp.concatenate([prev, cur], axis=1)}

    masks1 = masks(lambda i: i)
    sub_shift = SUB.bit_length() - 1
    masks4 = masks(lambda i: 4 * (i & (SUB - 1)) + (i >> sub_shift))

    def block(src, q_rows, k_rows, mask):
        gather = lambda idx, rws: jnp.concatenate([src[idx, r, :] for r in rws], axis=0)
        qs = jnp.concatenate([gather(Q_LO, q_rows), gather(Q_HI, q_rows)], axis=0)
        n_k = sum(r.stop - r.start for r in k_rows)
        s = _dot_nt(qs, gather(K_ALL, k_rows)) + mask[n_k]
        m = jnp.max(s, axis=-1, keepdims=True)
        p = jnp.exp2(s - m).astype(BF16)
        p = jnp.concatenate([p[:BAND], p[BAND:]], axis=1)
        reps = n_k // BAND
        rhs = jnp.concatenate(
            [jnp.concatenate([gather(V_LO, k_rows), jnp.concatenate([ones_lo] * reps, axis=0)], axis=1),
             jnp.concatenate([gather(V_HI, k_rows), jnp.concatenate([ones_hi] * reps, axis=0)], axis=1)],
            axis=0)
        r = _dot(p, rhs)
        return r[:, :LANES], r[:, LANES:], jnp.where(lo, m[:BAND], m[BAND:])

    def put(st, dst_rows, state):
        for i in range(3):
            st[i, dst_rows, :] = state[i]

    get = lambda st, r: [st[i, r, :] for i in range(3)]
    part = lambda state, j: [a[j * SUB:(j + 1) * SUB] for a in state]

    for cls in range(n_blk):
        put(st16, rows(cls * BAND), block(c16, [rows(cls * BAND)], [rows(cls * BAND)], masks1))

    for r4 in range(4):
        pieces = lambda kb: [rows((r4 + 4 * c) * BAND + kb * SUB, SUB) for c in range(4)]
        for kb in range(BAND // SUB):
            k_rows = pieces(kb) if kb == 0 else pieces(kb - 1) + pieces(kb)
            state = block(c16, pieces(kb), k_rows, masks4)
            for c, piece in enumerate(pieces(kb)):
                put(tmp, pl.ds(r4 * quarter + kb * BAND + c, SUB, stride=4),
                    _merge(part(state, c), get(st16, piece)))

    for blk in range(n_blk):
        pieces = lambda b: [rows(r4 * quarter + b * SUB, SUB) for r4 in range(4)]
        k_rows = pieces(blk) if blk == 0 else pieces(blk - 1) + pieces(blk)
        state = block(c4, pieces(blk), k_rows, masks4)
        for r4, piece in enumerate(pieces(blk)):
            acc, den, _ = _merge(part(state, r4), get(tmp, piece))
            y_ref[pl.ds(blk * BAND + r4, SUB, stride=4), :] = acc / den


def _attn(q, k, v, batch):
    t = q.shape[0]
    s_len = t // batch
    assert s_len == 16 * BAND and BAND == 4 * SUB
    slab = pl.BlockSpec((s_len, LANES), lambda b, j: (b, j))
    operands = pltpu.VMEM((5, s_len, LANES), BF16)
    state = pltpu.VMEM((3, s_len, LANES), F32)
    return pl.pallas_call(
        _attn_kernel,
        grid=(batch, D_ATT // LANES),
        in_specs=[slab, slab, slab],
        out_specs=slab,
        out_shape=jax.ShapeDtypeStruct((t, D_ATT), F32),
        scratch_shapes=[operands, operands, state, state],
        compiler_params=_params(2),
        name="dilated_attn",
    )(q, k, v)


def _mem_kv_kernel(m_ref, g_ref, wk_hbm, wv_hbm, k_ref, v_ref, wk_ref, wv_ref, stage, sem):
    @pl.when(pl.program_id(0) == 0)
    def _():
        _stream_weight(wk_hbm, wk_ref, stage, sem)
        _stream_weight(wv_hbm, wv_ref, stage, sem)

    mn = _rms(m_ref[...], g_ref[...]).astype(BF16)
    k_ref[...] = (_dot(mn, wk_ref[...]) * (X_HEAD_DIM ** -0.5)).astype(BF16)
    v_ref[...] = _dot(mn, wv_ref[...]).astype(BF16)


def _mem_kv(mem, g, wk, wv):
    b = mem.shape[0]
    hbm = pl.BlockSpec(memory_space=pl.ANY)
    per_batch = pl.BlockSpec((None, N_MEM, D_MODEL), lambda i: (i, 0, 0))
    weight = pltpu.VMEM((D_MODEL, D_MODEL), BF16)
    return pl.pallas_call(
        _mem_kv_kernel,
        grid=(b,),
        in_specs=[per_batch, _const_spec((1, D_MODEL)), hbm, hbm],
        out_specs=[per_batch, per_batch],
        out_shape=[jax.ShapeDtypeStruct((b, N_MEM, D_MODEL), BF16)] * 2,
        scratch_shapes=[weight, weight, pltpu.VMEM((2, W_STAGE_ROWS, D_MODEL), F32),
                        pltpu.SemaphoreType.DMA((2,))],
        compiler_params=_params(1),
        name="mem_kv",
    )(mem, g, wk, wv)


def _cross_kernel(x_ref, ya_ref, yb_ref, nb_ref, wmix_hbm, g_ref, wq_hbm, k_ref, v_ref, wo_hbm, out_ref,
                  wmix_ref, wq_ref, wo_ref, stage, sem):
    @pl.when((pl.program_id(0) == 0) & (pl.program_id(1) == 0))
    def _():
        _stream_weight(wmix_hbm, wmix_ref, stage, sem)
        _stream_weight(wq_hbm, wq_ref, stage, sem)
        _stream_weight(wo_hbm, wo_ref, stage, sem)

    yb = _rms(yb_ref[...], nb_ref[...]).astype(BF16)
    x = (x_ref[...] + _dot(ya_ref[...], wmix_ref[0:D_SGU, :])
         + _dot(yb, wmix_ref[D_SGU:D_SGU + D_ATT, :]))
    h = _rms(x, g_ref[...]).astype(BF16)
    q = _dot(h, wq_ref[...]).astype(BF16)
    heads = []
    for hh in range(X_HEADS):
        cols = slice(hh * X_HEAD_DIM, (hh + 1) * X_HEAD_DIM)
        s = _dot_nt(q[:, cols], k_ref[:, cols])
        p = jnp.exp(s - jnp.max(s, axis=-1, keepdims=True))
        l = jnp.sum(p, axis=-1, keepdims=True)
        heads.append((_dot(p.astype(BF16), v_ref[:, cols]) / l).astype(BF16))
    o = jnp.concatenate(heads, axis=1)
    out_ref[...] = x + _dot(o, wo_ref[...])


def _cross(x, ya, yb, norm_b, w_mix, g, wq, k, v, wo, batch):
    t = x.shape[0]
    per_batch = t // batch // CROSS_TILE
    full = pl.BlockSpec((CROSS_TILE, D_MODEL), lambda b, i: (b * per_batch + i, 0))
    half = pl.BlockSpec((CROSS_TILE, D_ATT), lambda b, i: (b * per_batch + i, 0))
    mem_rows = pl.BlockSpec((None, N_MEM, D_MODEL), lambda b, i: (b, 0, 0))
    hbm = pl.BlockSpec(memory_space=pl.ANY)
    weight = pltpu.VMEM((D_MODEL, D_MODEL), BF16)
    return pl.pallas_call(
        _cross_kernel,
        grid=(batch, per_batch),
        in_specs=[full, half, half, _const_spec((1, D_ATT)), hbm, _const_spec((1, D_MODEL)), hbm,
                  mem_rows, mem_rows, hbm],
        out_specs=full,
        out_shape=jax.ShapeDtypeStruct((t, D_MODEL), F32),
        scratch_shapes=[weight, weight, weight, pltpu.VMEM((2, W_STAGE_ROWS, D_MODEL), F32),
                        pltpu.SemaphoreType.DMA((2,))],
        compiler_params=_params(2),
        name="mix_out_cross",
    )(x, ya, yb, norm_b, w_mix, g, wq, k, v, wo)


def _rope_lane_freq():
    half = ROPE_DIM // 2
    inv_freq = ROPE_THETA ** (-2.0 * jnp.arange(half, dtype=F32) / ROPE_DIM)
    e = jnp.arange(LANES) % HEAD_DIM
    return jnp.where(e < ROPE_DIM, inv_freq[e % half], 0.0).astype(F32).reshape(1, LANES)


def _group_mean_matrix():
    g = np.arange(D_SGU) // SGU_GROUP_DIM
    return jnp.asarray(np.where(g[:, None] == g[None, :], 1.0 / SGU_GROUP_DIM, 0.0), dtype=BF16)


def kernel(x, mem, positions, ffn1_norm, ffn1_w_gate, ffn1_w_up, ffn1_w_down, mix_norm, w_in, sgu_ln_g, sgu_ln_b, sgu_w_s, sgu_b_s, out_norm_a, out_norm_b, w_out, cross_norm, mem_norm, cross_wq, cross_wk, cross_wv, cross_wo, ffn2_norm, ffn2_w_gate, ffn2_w_up, ffn2_w_down, final_norm):
    b, s, d = x.shape
    t = b * s
    depth = ffn1_norm.shape[0]
    row = lambda p: p.reshape(1, -1).astype(F32)
    xt = x.reshape(t, d)
    pos = positions.reshape(t, 1)
    freq = _rope_lane_freq()
    gmat = _group_mean_matrix()
    fin = row(final_norm)
    for l in range(depth):
        xt = _ffn(xt, row(ffn1_norm[l]), ffn1_w_gate[l], ffn1_w_up[l], ffn1_w_down[l], fin,
                  final_norm=False)

        ya, q, k, v = _in_proj(xt, pos, row(mix_norm[l]), w_in[l], row(sgu_ln_g[l]),
                               row(sgu_ln_b[l]), sgu_w_s[l], sgu_b_s[l].T, gmat, freq,
                               row(out_norm_a[l]))
        yb = _attn(q, k, v, b)
        km, vm = _mem_kv(mem, row(mem_norm[l]), cross_wk[l], cross_wv[l])
        xt = _cross(xt, ya, yb, row(out_norm_b[l]), w_out[l], row(cross_norm[l]),
                    cross_wq[l], km, vm, cross_wo[l], b)

        xt = _ffn(xt, row(ffn2_norm[l]), ffn2_w_gate[l], ffn2_w_up[l], ffn2_w_down[l], fin,
                  final_norm=(l == depth - 1))
    return xt.reshape(b, s, d)
```

```python
import functools

import numpy as np
import jax
import jax.numpy as jnp
from jax import lax
from jax.experimental import pallas as pl
from jax.experimental.pallas import tpu as pltpu

F32 = jnp.float32
BF16 = jnp.bfloat16

D_MODEL = 1024
N_MEM = 256
D_SGU = 512
D_ATT = 512
SGU_GROUPS = 8
SGU_GROUP_DIM = D_SGU // SGU_GROUPS
CHUNK = 128
N_HEADS = 8
HEAD_DIM = D_ATT // N_HEADS
BAND = 128
ROPE_THETA = 500000.0
ROPE_DIM = HEAD_DIM // 4
X_HEADS = 4
X_HEAD_DIM = D_MODEL // X_HEADS
D_FF = 2816
NORM_EPS = 1e-6
LN_EPS = 1e-5
LOG2_E = 1.4426950408889634

LANES = 128
TOKEN_TILE = 512
CROSS_TILE = 1024
MEM_BATCH = 4
FF_CHUNK = 512
W_STAGE_ROWS = 256
VMEM_LIMIT_BYTES = 56 * 1024 * 1024


def _rms(x, g):
    return x * lax.rsqrt(jnp.mean(x * x, axis=-1, keepdims=True) + NORM_EPS) * g


def _dot(a, b):
    return jnp.dot(a, b, preferred_element_type=F32)


def _dot_nt(a, b):
    return lax.dot_general(a, b, (((1,), (1,)), ((), ())), preferred_element_type=F32)


def _const_spec(shape):
    zeros = (0,) * len(shape)
    return pl.BlockSpec(shape, lambda *_: zeros, pipeline_mode=pl.Buffered(1))


def _stream_weight(w_hbm, w_vmem, stage, sem):
    rows, cols = w_vmem.shape
    chunk = stage.shape[1]
    assert rows % chunk == 0

    def copy(c):
        return pltpu.make_async_copy(w_hbm.at[pl.ds(c * chunk, chunk), :],
                                     stage.at[c % 2, :, pl.ds(0, cols)], sem.at[c % 2])

    copy(0).start()
    for c in range(rows // chunk):
        if c + 1 < rows // chunk:
            copy(c + 1).start()
        copy(c).wait()
        w_vmem[c * chunk:(c + 1) * chunk, :] = stage[c % 2, :, 0:cols].astype(BF16)


def _params(n_axes, **extra):
    return pltpu.CompilerParams(
        dimension_semantics=("arbitrary",) * n_axes,
        vmem_limit_bytes=VMEM_LIMIT_BYTES,
        **extra,
    )


FF_CHUNKS = tuple((c0, min(FF_CHUNK, D_FF - c0)) for c0 in range(0, D_FF, FF_CHUNK))


def _ffn_kernel(x_ref, g_ref, wg_hbm, wu_hbm, wd_hbm, fg_ref, o_ref,
                wg, wu, wd, stage_g, stage_u, stage_d, sem, *, final_norm):
    def chunk_copies(c, slot):
        c0, ck = FF_CHUNKS[c]
        return (
            pltpu.make_async_copy(wg_hbm.at[:, pl.ds(c0, ck)], stage_g.at[slot, :, pl.ds(0, ck)], sem.at[slot, 0]),
            pltpu.make_async_copy(wu_hbm.at[:, pl.ds(c0, ck)], stage_u.at[slot, :, pl.ds(0, ck)], sem.at[slot, 1]),
            pltpu.make_async_copy(wd_hbm.at[pl.ds(c0, ck), :], stage_d.at[slot, pl.ds(0, ck), :], sem.at[slot, 2]),
        )

    def run(load_weights):
        if load_weights:
            for cp in chunk_copies(0, 0):
                cp.start()
        x = x_ref[...]
        h = _rms(x, g_ref[...]).astype(BF16)
        acc = jnp.zeros(x.shape, F32)
        for c, (c0, ck) in enumerate(FF_CHUNKS):
            if load_weights:
                slot = c % 2
                if c + 1 < len(FF_CHUNKS):
                    for cp in chunk_copies(c + 1, 1 - slot):
                        cp.start()
                for cp in chunk_copies(c, slot):
                    cp.wait()
                wg[:, c0:c0 + ck] = stage_g[slot, :, 0:ck].astype(BF16)
                wu[:, c0:c0 + ck] = stage_u[slot, :, 0:ck].astype(BF16)
                wd[c0:c0 + ck, :] = stage_d[slot, 0:ck, :].astype(BF16)
            g = _dot(h, wg[:, c0:c0 + ck])
            u = _dot(h, wu[:, c0:c0 + ck])
            a = (g * jax.nn.sigmoid(g)) * u
            acc = acc + _dot(a.astype(BF16), wd[c0:c0 + ck, :])
        y = x + 0.5 * acc
        if final_norm:
            y = _rms(y, fg_ref[...])
        o_ref[...] = y

    first = pl.program_id(0) == 0
    pl.when(first)(functools.partial(run, True))
    pl.when(jnp.logical_not(first))(functools.partial(run, False))


def _ffn(x, g, wg, wu, wd, fg, *, final_norm):
    t = x.shape[0]
    tile = pl.BlockSpec((TOKEN_TILE, D_MODEL), lambda i: (i, 0))
    hbm = pl.BlockSpec(memory_space=pl.ANY)
    return pl.pallas_call(
        functools.partial(_ffn_kernel, final_norm=final_norm),
        grid=(t // TOKEN_TILE,),
        in_specs=[tile, _const_spec((1, D_MODEL)), hbm, hbm, hbm, _const_spec((1, D_MODEL))],
        out_specs=tile,
        out_shape=jax.ShapeDtypeStruct((t, D_MODEL), F32),
        scratch_shapes=[pltpu.VMEM((D_MODEL, D_FF), BF16), pltpu.VMEM((D_MODEL, D_FF), BF16),
                        pltpu.VMEM((D_FF, D_MODEL), BF16),
                        pltpu.VMEM((2, D_MODEL, FF_CHUNK), F32), pltpu.VMEM((2, D_MODEL, FF_CHUNK), F32),
                        pltpu.VMEM((2, FF_CHUNK, D_MODEL), F32),
                        pltpu.SemaphoreType.DMA((2, 3))],
        compiler_params=_params(1),
        name="ffn_final" if final_norm else "ffn",
    )(x, g, wg, wu, wd, fg)


def _gelu(x):
    return 0.5 * x * (1.0 + lax.erf(x * (2.0 ** -0.5)))


def _group_mean(x, gmat):
    return _dot(x.astype(BF16), gmat)


def _rotary(t, cos, sin_signed, first_half):
    parts = []
    for j in range(D_ATT // LANES):
        tj = t[:, j * LANES:(j + 1) * LANES]
        partner = jnp.where(first_half,
                            pltpu.roll(tj, LANES - ROPE_DIM // 2, axis=1),
                            pltpu.roll(tj, ROPE_DIM // 2, axis=1))
        parts.append(tj * cos + partner * sin_signed)
    return jnp.concatenate(parts, axis=1)


def _in_proj_kernel(x_ref, pos_ref, g_ref, win_hbm, lng_ref, lnb_ref, ws_ref, bst_ref,
                    gmat_ref, freq_ref, na_ref, ya_ref, q_ref, k_ref, v_ref,
                    ya_scr, win_ref, stage, sem):
    @pl.when(pl.program_id(0) == 0)
    def _():
        _stream_weight(win_hbm, win_ref, stage, sem)

    tm = x_ref.shape[0]
    n_chunks = tm // CHUNK
    h = _rms(x_ref[...], g_ref[...]).astype(BF16)

    def proj(c0, width):
        return _dot(h, win_ref[:, c0:c0 + width])

    u = _gelu(proj(0, D_SGU))
    v = _gelu(proj(D_SGU, D_SGU))
    gmat = gmat_ref[...]
    d = v - _group_mean(v, gmat)
    var = _group_mean(d * d, gmat)
    vln = d * lax.rsqrt(var + LN_EPS) * lng_ref[...] + lnb_ref[...]

    lane = lax.broadcasted_iota(jnp.int32, (CHUNK, LANES), 1)
    low_group = lane < SGU_GROUP_DIM
    row = lax.broadcasted_iota(jnp.int32, (CHUNK, CHUNK), 0)
    col = lax.broadcasted_iota(jnp.int32, (CHUNK, CHUNK), 1)
    causal = col <= row
    bst = bst_ref[...]
    for j in range(D_SGU // LANES):
        g0, g1 = 2 * j, 2 * j + 1
        w0 = jnp.where(causal, ws_ref[g0], 0.0)
        w1 = jnp.where(causal, ws_ref[g1], 0.0)
        wcat = jnp.concatenate([w0, w1], axis=1).astype(BF16)
        vj = vln[:, j * LANES:(j + 1) * LANES]
        top = jnp.concatenate(
            [jnp.where(low_group, vj[c * CHUNK:(c + 1) * CHUNK], 0.0) for c in range(n_chunks)], axis=1)
        bot = jnp.concatenate(
            [jnp.where(low_group, 0.0, vj[c * CHUNK:(c + 1) * CHUNK]) for c in range(n_chunks)], axis=1)
        rhs = jnp.concatenate([top, bot], axis=0).astype(BF16)
        mixed = _dot(wcat, rhs)
        bias = jnp.where(low_group,
                         jnp.broadcast_to(bst[:, g0:g0 + 1], (CHUNK, LANES)),
                         jnp.broadcast_to(bst[:, g1:g1 + 1], (CHUNK, LANES)))
        for c in range(n_chunks):
            uj = u[c * CHUNK:(c + 1) * CHUNK, j * LANES:(j + 1) * LANES]
            ya_scr[c * CHUNK:(c + 1) * CHUNK, j * LANES:(j + 1) * LANES] = (
                uj * (mixed[:, c * LANES:(c + 1) * LANES] + bias))
    ya_ref[...] = _rms(ya_scr[...], na_ref[...]).astype(BF16)

    ang = pos_ref[...].astype(F32) * freq_ref[...]
    cos = jnp.cos(ang)
    sin = jnp.sin(ang)
    lane_t = lax.broadcasted_iota(jnp.int32, (tm, LANES), 1)
    first_half = (lane_t & (HEAD_DIM - 1)) < ROPE_DIM // 2
    sin_signed = jnp.where(first_half, -sin, sin)
    q = proj(2 * D_SGU, D_ATT) * (HEAD_DIM ** -0.5 * LOG2_E)
    q_ref[...] = _rotary(q, cos, sin_signed, first_half)
    k = proj(2 * D_SGU + D_ATT, D_ATT)
    k_ref[...] = _rotary(k, cos, sin_signed, first_half)
    v_ref[...] = proj(2 * D_SGU + 2 * D_ATT, D_ATT)


def _in_proj(x, pos, g, w_in, ln_g, ln_b, w_s, b_st, gmat, freq, norm_a):
    t = x.shape[0]
    d_in = w_in.shape[1]
    half = pl.BlockSpec((TOKEN_TILE, D_SGU), lambda i: (i, 0))
    return pl.pallas_call(
        _in_proj_kernel,
        grid=(t // TOKEN_TILE,),
        in_specs=[pl.BlockSpec((TOKEN_TILE, D_MODEL), lambda i: (i, 0)),
                  pl.BlockSpec((TOKEN_TILE, LANES), lambda i: (i, 0)),
                  _const_spec((1, D_MODEL)), pl.BlockSpec(memory_space=pl.ANY),
                  _const_spec((1, D_SGU)), _const_spec((1, D_SGU)),
                  _const_spec((SGU_GROUPS, CHUNK, CHUNK)), _const_spec((CHUNK, SGU_GROUPS)),
                  _const_spec((D_SGU, D_SGU)), _const_spec((1, LANES)), _const_spec((1, D_SGU))],
        out_specs=[half, half, half, half],
        out_shape=[jax.ShapeDtypeStruct((t, D_SGU), BF16)] + [jax.ShapeDtypeStruct((t, D_ATT), F32)] * 3,
        scratch_shapes=[pltpu.VMEM((TOKEN_TILE, D_SGU), F32), pltpu.VMEM((D_MODEL, d_in), BF16),
                        pltpu.VMEM((2, W_STAGE_ROWS, d_in), F32), pltpu.SemaphoreType.DMA((2,))],
        compiler_params=_params(1),
        name="in_proj",
    )(x, pos, g, w_in, ln_g, ln_b, w_s, b_st, gmat, freq, norm_a)


Q_LO, Q_HI, K_ALL, V_LO, V_HI = range(5)
ACC, DEN, MAX = range(3)
SUB = BAND // 4


def _merge(a, b):
    m = jnp.maximum(a[MAX], b[MAX])
    w_a = jnp.exp2(a[MAX] - m)
    w_b = jnp.exp2(b[MAX] - m)
    return w_a * a[ACC] + w_b * b[ACC], w_a * a[DEN] + w_b * b[DEN], m


def _attn_kernel(q_ref, k_ref, v_ref, y_ref, c4, c16, tmp, st16):
    s_len = q_ref.shape[0]
    n_blk = s_len // BAND
    quarter = s_len // 4
    rows = lambda start, n=BAND: slice(start, start + n)
    lane = lax.broadcasted_iota(jnp.int32, (BAND, LANES), 1)
    lo = lane < HEAD_DIM
    ones_lo = jnp.where(lo, 1.0, 0.0).astype(BF16)
    ones_hi = jnp.where(lo, 0.0, 1.0).astype(BF16)

    def fill(dst, blk, qkv):
        q, k, v = qkv
        dst_rows = rows(blk * BAND)
        dst[Q_LO, dst_rows, :] = jnp.where(lo, q, 0.0).astype(BF16)
        dst[Q_HI, dst_rows, :] = jnp.where(lo, 0.0, q).astype(BF16)
        dst[K_ALL, dst_rows, :] = k.astype(BF16)
        dst[V_LO, dst_rows, :] = jnp.where(lo, v, 0.0).astype(BF16)
        dst[V_HI, dst_rows, :] = jnp.where(lo, 0.0, v).astype(BF16)

    srcs = (q_ref, k_ref, v_ref)
    for r4 in range(4):
        for kb in range(quarter // BAND):
            for i, src in enumerate(srcs):
                tmp[i, rows(r4 * quarter + kb * BAND), :] = src[pl.ds(r4 + 4 * BAND * kb, BAND, stride=4), :]
    for blk in range(n_blk):
        fill(c4, blk, [tmp[i, rows(blk * BAND), :] for i in range(3)])
    for cls in range(n_blk):
        r4, c = cls % 4, cls // 4
        fill(c16, cls, [tmp[i, pl.ds(r4 * quarter + c, BAND, stride=4), :] for i in range(3)])

    qi = lax.broadcasted_iota(jnp.int32, (2 * BAND, BAND), 0) & (BAND - 1)
    kj = lax.broadcasted_iota(jnp.int32, (2 * BAND, BAND), 1)

    def masks(pos):
        neg = jnp.float32(-jnp.inf)
        prev = jnp.where(pos(kj) >= pos(qi), 0.0, neg)
        cur = jnp.where(pos(kj) <= pos(qi), 0.0, neg)
        return {BAND: cur, 2 * BAND: jnp.concatenate([prev, cur], axis=1)}

    masks1 = masks(lambda i: i)
    sub_shift = SUB.bit_length() - 1
    masks4 = masks(lambda i: 4 * (i & (SUB - 1)) + (i >> sub_shift))

    def block(src, q_rows, k_rows, mask):
        gather = lambda idx, rws: jnp.concatenate([src[idx, r, :] for r in rws], axis=0)
        qs = jnp.concatenate([gather(Q_LO, q_rows), gather(Q_HI, q_rows)], axis=0)
        n_k = sum(r.stop - r.start for r in k_rows)
        s = _dot_nt(qs, gather(K_ALL, k_rows)) + mask[n_k]
        m = jnp.max(s, axis=-1, keepdims=True)
        p = jnp.exp2(s - m).astype(BF16)
        p = jnp.concatenate([p[:BAND], p[BAND:]], axis=1)
        reps = n_k // BAND
        rhs = jnp.concatenate(
            [jnp.concatenate([gather(V_LO, k_rows), jnp.concatenate([ones_lo] * reps, axis=0)], axis=1),
             jnp.concatenate([gather(V_HI, k_rows), jnp.concatenate([ones_hi] * reps, axis=0)], axis=1)],
            axis=0)
        r = _dot(p, rhs)
        return r[:, :LANES], r[:, LANES:], jnp.where(lo, m[:BAND], m[BAND:])

    def put(st, dst_rows, state):
        for i in range(3):
            st[i, dst_rows, :] = state[i]

    get = lambda st, r: [st[i, r, :] for i in range(3)]
    part = lambda state, j: [a[j * SUB:(j + 1) * SUB] for a in state]

    for cls in range(n_blk):
        put(st16, rows(cls * BAND), block(c16, [rows(cls * BAND)], [rows(cls * BAND)], masks1))

    for r4 in range(4):
        pieces = lambda kb: [rows((r4 + 4 * c) * BAND + kb * SUB, SUB) for c in range(4)]
        for kb in range(BAND // SUB):
            k_rows = pieces(kb) if kb == 0 else pieces(kb - 1) + pieces(kb)
            state = block(c16, pieces(kb), k_rows, masks4)
            for c, piece in enumerate(pieces(kb)):
                put(tmp, pl.ds(r4 * quarter + kb * BAND + c, SUB, stride=4),
                    _merge(part(state, c), get(st16, piece)))

    for blk in range(n_blk):
        pieces = lambda b: [rows(r4 * quarter + b * SUB, SUB) for r4 in range(4)]
        k_rows = pieces(blk) if blk == 0 else pieces(blk - 1) + pieces(blk)
        state = block(c4, pieces(blk), k_rows, masks4)
        for r4, piece in enumerate(pieces(blk)):
            acc, den, _ = _merge(part(state, r4), get(tmp, piece))
            y_ref[pl.ds(blk * BAND + r4, SUB, stride=4), :] = acc / den


def _attn(q, k, v, batch):
    t = q.shape[0]
    s_len = t // batch
    assert s_len == 16 * BAND and BAND == 4 * SUB
    slab = pl.BlockSpec((s_len, LANES), lambda b, j: (b, j))
    operands = pltpu.VMEM((5, s_len, LANES), BF16)
    state = pltpu.VMEM((3, s_len, LANES), F32)
    return pl.pallas_call(
        _attn_kernel,
        grid=(batch, D_ATT // LANES),
        in_specs=[slab, slab, slab],
        out_specs=slab,
        out_shape=jax.ShapeDtypeStruct((t, D_ATT), F32),
        scratch_shapes=[operands, operands, state, state],
        compiler_params=_params(2),
        name="dilated_attn",
    )(q, k, v)


def _mem_kv_kernel(m_ref, g_ref, wk_hbm, wv_hbm, k_ref, v_ref, wk_ref, wv_ref, stage, sem):
    @pl.when(pl.program_id(0) == 0)
    def _():
        _stream_weight(wk_hbm, wk_ref, stage, sem)
        _stream_weight(wv_hbm, wv_ref, stage, sem)

    shape = m_ref.shape
    mn = _rms(m_ref[...].reshape(-1, D_MODEL), g_ref[...]).astype(BF16)
    k = _dot(mn, wk_ref[...]) * (X_HEAD_DIM ** -0.5)
    k_ref[...] = k.astype(BF16).reshape(shape)
    v_ref[...] = _dot(mn, wv_ref[...]).astype(BF16).reshape(shape)


def _mem_kv(mem, g, wk, wv):
    b = mem.shape[0]
    assert b % MEM_BATCH == 0
    hbm = pl.BlockSpec(memory_space=pl.ANY)
    rows = pl.BlockSpec((MEM_BATCH, N_MEM, D_MODEL), lambda i: (i, 0, 0))
    weight = pltpu.VMEM((D_MODEL, D_MODEL), BF16)
    return pl.pallas_call(
        _mem_kv_kernel,
        grid=(b // MEM_BATCH,),
        in_specs=[rows, _const_spec((1, D_MODEL)), hbm, hbm],
        out_specs=[rows, rows],
        out_shape=[jax.ShapeDtypeStruct((b, N_MEM, D_MODEL), BF16)] * 2,
        scratch_shapes=[weight, weight, pltpu.VMEM((2, W_STAGE_ROWS, D_MODEL), F32),
                        pltpu.SemaphoreType.DMA((2,))],
        compiler_params=_params(1),
        name="mem_kv",
    )(mem, g, wk, wv)


def _cross_kernel(x_ref, ya_ref, yb_ref, nb_ref, wmix_hbm, g_ref, wq_hbm, k_ref, v_ref, wo_hbm, out_ref,
                  wmix_ref, wq_ref, wo_ref, stage, sem):
    @pl.when((pl.program_id(0) == 0) & (pl.program_id(1) == 0))
    def _():
        _stream_weight(wmix_hbm, wmix_ref, stage, sem)
        _stream_weight(wq_hbm, wq_ref, stage, sem)
        _stream_weight(wo_hbm, wo_ref, stage, sem)

    yb = _rms(yb_ref[...], nb_ref[...]).astype(BF16)
    x = (x_ref[...] + _dot(ya_ref[...], wmix_ref[0:D_SGU, :])
         + _dot(yb, wmix_ref[D_SGU:D_SGU + D_ATT, :]))
    h = _rms(x, g_ref[...]).astype(BF16)
    q = _dot(h, wq_ref[...]).astype(BF16)
    heads = []
    for hh in range(X_HEADS):
        cols = slice(hh * X_HEAD_DIM, (hh + 1) * X_HEAD_DIM)
        s = _dot_nt(q[:, cols], k_ref[:, cols])
        p = jnp.exp(s - jnp.max(s, axis=-1, keepdims=True))
        l = jnp.sum(p, axis=-1, keepdims=True)
        heads.append((_dot(p.astype(BF16), v_ref[:, cols]) / l).astype(BF16))
    o = jnp.concatenate(heads, axis=1)
    out_ref[...] = x + _dot(o, wo_ref[...])


def _cross(x, ya, yb, norm_b, w_mix, g, wq, k, v, wo, batch):
    t = x.shape[0]
    per_batch = t // batch // CROSS_TILE
    full = pl.BlockSpec((CROSS_TILE, D_MODEL), lambda b, i: (b * per_batch + i, 0))
    half = pl.BlockSpec((CROSS_TILE, D_ATT), lambda b, i: (b * per_batch + i, 0))
    mem_rows = pl.BlockSpec((None, N_MEM, D_MODEL), lambda b, i: (b, 0, 0))
    hbm = pl.BlockSpec(memory_space=pl.ANY)
    weight = pltpu.VMEM((D_MODEL, D_MODEL), BF16)
    return pl.pallas_call(
        _cross_kernel,
        grid=(batch, per_batch),
        in_specs=[full, half, half, _const_spec((1, D_ATT)), hbm, _const_spec((1, D_MODEL)), hbm,
                  mem_rows, mem_rows, hbm],
        out_specs=full,
        out_shape=jax.ShapeDtypeStruct((t, D_MODEL), F32),
        scratch_shapes=[weight, weight, weight, pltpu.VMEM((2, W_STAGE_ROWS, D_MODEL), F32),
                        pltpu.SemaphoreType.DMA((2,))],
        compiler_params=_params(2),
        name="mix_out_cross",
    )(x, ya, yb, norm_b, w_mix, g, wq, k, v, wo)


def _rope_lane_freq():
    half = ROPE_DIM // 2
    inv_freq = ROPE_THETA ** (-2.0 * jnp.arange(half, dtype=F32) / ROPE_DIM)
    e = jnp.arange(LANES) % HEAD_DIM
    return jnp.where(e < ROPE_DIM, inv_freq[e % half], 0.0).astype(F32).reshape(1, LANES)


def _group_mean_matrix():
    g = np.arange(D_SGU) // SGU_GROUP_DIM
    return jnp.asarray(np.where(g[:, None] == g[None, :], 1.0 / SGU_GROUP_DIM, 0.0), dtype=BF16)


def kernel(x, mem, positions, ffn1_norm, ffn1_w_gate, ffn1_w_up, ffn1_w_down, mix_norm, w_in, sgu_ln_g, sgu_ln_b, sgu_w_s, sgu_b_s, out_norm_a, out_norm_b, w_out, cross_norm, mem_norm, cross_wq, cross_wk, cross_wv, cross_wo, ffn2_norm, ffn2_w_gate, ffn2_w_up, ffn2_w_down, final_norm):
    b, s, d = x.shape
    t = b * s
    depth = ffn1_norm.shape[0]
    row = lambda p: p.reshape(1, -1).astype(F32)
    xt = x.reshape(t, d)
    pos = jnp.broadcast_to(positions.reshape(t, 1), (t, LANES))
    freq = _rope_lane_freq()
    gmat = _group_mean_matrix()
    fin = row(final_norm)
    for l in range(depth):
        xt = _ffn(xt, row(ffn1_norm[l]), ffn1_w_gate[l], ffn1_w_up[l], ffn1_w_down[l], fin,
                  final_norm=False)

        ya, q, k, v = _in_proj(xt, pos, row(mix_norm[l]), w_in[l], row(sgu_ln_g[l]),
                               row(sgu_ln_b[l]), sgu_w_s[l], sgu_b_s[l].T, gmat, freq,
                               row(out_norm_a[l]))
        yb = _attn(q, k, v, b)
        km, vm = _mem_kv(mem, row(mem_norm[l]), cross_wk[l], cross_wv[l])
        xt = _cross(xt, ya, yb, row(out_norm_b[l]), w_out[l], row(cross_norm[l]),
                    cross_wq[l], km, vm, cross_wo[l], b)

        xt = _ffn(xt, row(ffn2_norm[l]), ffn2_w_gate[l], ffn2_w_up[l], ffn2_w_down[l], fin,
                  final_norm=(l == depth - 1))
    return xt.reshape(b, s, d)
```

```python
import functools

import numpy as np
import jax
import jax.numpy as jnp
from jax import lax
from jax.experimental import pallas as pl
from jax.experimental.pallas import tpu as pltpu

F32 = jnp.float32
BF16 = jnp.bfloat16

D_MODEL = 1024
N_MEM = 256
D_SGU = 512
D_ATT = 512
SGU_GROUPS = 8
SGU_GROUP_DIM = D_SGU // SGU_GROUPS
CHUNK = 128
N_HEADS = 8
HEAD_DIM = D_ATT // N_HEADS
BAND = 128
ROPE_THETA = 500000.0
ROPE_DIM = HEAD_DIM // 4
X_HEADS = 4
X_HEAD_DIM = D_MODEL // X_HEADS
D_FF = 2816
NORM_EPS = 1e-6
LN_EPS = 1e-5
LOG2_E = 1.4426950408889634

LANES = 128
TOKEN_TILE = 512
CROSS_TILE = 1024
MEM_BATCH = 4
FF_CHUNK = 512
W_STAGE_ROWS = 256
VMEM_LIMIT_BYTES = 56 * 1024 * 1024


def _rms(x, g):
    return x * lax.rsqrt(jnp.mean(x * x, axis=-1, keepdims=True) + NORM_EPS) * g


def _dot(a, b):
    return jnp.dot(a, b, preferred_element_type=F32)


def _dot_nt(a, b):
    return lax.dot_general(a, b, (((1,), (1,)), ((), ())), preferred_element_type=F32)


def _const_spec(shape):
    zeros = (0,) * len(shape)
    return pl.BlockSpec(shape, lambda *_: zeros, pipeline_mode=pl.Buffered(1))


def _stream_weight(w_hbm, w_vmem, stage, sem):
    rows, cols = w_vmem.shape
    chunk = stage.shape[1]
    assert rows % chunk == 0

    def copy(c):
        return pltpu.make_async_copy(w_hbm.at[pl.ds(c * chunk, chunk), :],
                                     stage.at[c % 2, :, pl.ds(0, cols)], sem.at[c % 2])

    copy(0).start()
    for c in range(rows // chunk):
        if c + 1 < rows // chunk:
            copy(c + 1).start()
        copy(c).wait()
        w_vmem[c * chunk:(c + 1) * chunk, :] = stage[c % 2, :, 0:cols].astype(BF16)


def _params(n_axes, **extra):
    return pltpu.CompilerParams(
        dimension_semantics=("arbitrary",) * n_axes,
        vmem_limit_bytes=VMEM_LIMIT_BYTES,
        **extra,
    )


FF_CHUNKS = tuple((c0, min(FF_CHUNK, D_FF - c0)) for c0 in range(0, D_FF, FF_CHUNK))


def _ffn_kernel(x_ref, g_ref, wg_hbm, wu_hbm, wd_hbm, fg_ref, o_ref,
                wg, wu, wd, stage_g, stage_u, stage_d, sem, *, final_norm):
    def chunk_copies(c, slot):
        c0, ck = FF_CHUNKS[c]
        return (
            pltpu.make_async_copy(wg_hbm.at[:, pl.ds(c0, ck)], stage_g.at[slot, :, pl.ds(0, ck)], sem.at[slot, 0]),
            pltpu.make_async_copy(wu_hbm.at[:, pl.ds(c0, ck)], stage_u.at[slot, :, pl.ds(0, ck)], sem.at[slot, 1]),
            pltpu.make_async_copy(wd_hbm.at[pl.ds(c0, ck), :], stage_d.at[slot, pl.ds(0, ck), :], sem.at[slot, 2]),
        )

    def run(load_weights):
        if load_weights:
            for cp in chunk_copies(0, 0):
                cp.start()
        x = x_ref[...]
        h = _rms(x, g_ref[...]).astype(BF16)
        acc = jnp.zeros(x.shape, F32)
        for c, (c0, ck) in enumerate(FF_CHUNKS):
            if load_weights:
                slot = c % 2
                if c + 1 < len(FF_CHUNKS):
                    for cp in chunk_copies(c + 1, 1 - slot):
                        cp.start()
                for cp in chunk_copies(c, slot):
                    cp.wait()
                wg[:, c0:c0 + ck] = stage_g[slot, :, 0:ck].astype(BF16)
                wu[:, c0:c0 + ck] = stage_u[slot, :, 0:ck].astype(BF16)
                wd[c0:c0 + ck, :] = stage_d[slot, 0:ck, :].astype(BF16)
            g = _dot(h, wg[:, c0:c0 + ck])
            u = _dot(h, wu[:, c0:c0 + ck])
            a = (g * jax.nn.sigmoid(g)) * u
            acc = acc + _dot(a.astype(BF16), wd[c0:c0 + ck, :])
        y = x + 0.5 * acc
        if final_norm:
            y = _rms(y, fg_ref[...])
        o_ref[...] = y

    first = pl.program_id(0) == 0
    pl.when(first)(functools.partial(run, True))
    pl.when(jnp.logical_not(first))(functools.partial(run, False))


def _ffn(x, g, wg, wu, wd, fg, *, final_norm):
    t = x.shape[0]
    tile = pl.BlockSpec((TOKEN_TILE, D_MODEL), lambda i: (i, 0))
    hbm = pl.BlockSpec(memory_space=pl.ANY)
    return pl.pallas_call(
        functools.partial(_ffn_kernel, final_norm=final_norm),
        grid=(t // TOKEN_TILE,),
        in_specs=[tile, _const_spec((1, D_MODEL)), hbm, hbm, hbm, _const_spec((1, D_MODEL))],
        out_specs=tile,
        out_shape=jax.ShapeDtypeStruct((t, D_MODEL), F32),
        scratch_shapes=[pltpu.VMEM((D_MODEL, D_FF), BF16), pltpu.VMEM((D_MODEL, D_FF), BF16),
                        pltpu.VMEM((D_FF, D_MODEL), BF16),
                        pltpu.VMEM((2, D_MODEL, FF_CHUNK), F32), pltpu.VMEM((2, D_MODEL, FF_CHUNK), F32),
                        pltpu.VMEM((2, FF_CHUNK, D_MODEL), F32),
                        pltpu.SemaphoreType.DMA((2, 3))],
        compiler_params=_params(1),
        name="ffn_final" if final_norm else "ffn",
    )(x, g, wg, wu, wd, fg)


def _gelu(x):
    return 0.5 * x * (1.0 + lax.erf(x * (2.0 ** -0.5)))


def _group_mean(x, gmat):
    return _dot(x.astype(BF16), gmat)


def _rotary(t, cos, sin_signed, first_half):
    parts = []
    for j in range(D_ATT // LANES):
        tj = t[:, j * LANES:(j + 1) * LANES]
        partner = jnp.where(first_half,
                            pltpu.roll(tj, LANES - ROPE_DIM // 2, axis=1),
                            pltpu.roll(tj, ROPE_DIM // 2, axis=1))
        parts.append(tj * cos + partner * sin_signed)
    return jnp.concatenate(parts, axis=1)


def _in_proj_kernel(x_ref, pos_ref, g_ref, win_hbm, lng_ref, lnb_ref, ws_ref, bst_ref,
                    gmat_ref, freq_ref, na_ref, ya_ref, q_ref, k_ref, v_ref,
                    ya_scr, win_ref, stage, sem):
    @pl.when(pl.program_id(0) == 0)
    def _():
        _stream_weight(win_hbm, win_ref, stage, sem)

    tm = x_ref.shape[0]
    n_chunks = tm // CHUNK
    h = _rms(x_ref[...], g_ref[...]).astype(BF16)

    def proj(c0, width):
        return _dot(h, win_ref[:, c0:c0 + width])

    u = _gelu(proj(0, D_SGU))
    v = _gelu(proj(D_SGU, D_SGU))
    gmat = gmat_ref[...]
    d = v - _group_mean(v, gmat)
    var = _group_mean(d * d, gmat)
    vln = d * lax.rsqrt(var + LN_EPS) * lng_ref[...] + lnb_ref[...]

    lane = lax.broadcasted_iota(jnp.int32, (CHUNK, LANES), 1)
    low_group = lane < SGU_GROUP_DIM
    row = lax.broadcasted_iota(jnp.int32, (CHUNK, CHUNK), 0)
    col = lax.broadcasted_iota(jnp.int32, (CHUNK, CHUNK), 1)
    causal = col <= row
    bst = bst_ref[...]
    for j in range(D_SGU // LANES):
        g0, g1 = 2 * j, 2 * j + 1
        w0 = jnp.where(causal, ws_ref[g0], 0.0)
        w1 = jnp.where(causal, ws_ref[g1], 0.0)
        wcat = jnp.concatenate([w0, w1], axis=1).astype(BF16)
        vj = vln[:, j * LANES:(j + 1) * LANES]
        top = jnp.concatenate(
            [jnp.where(low_group, vj[c * CHUNK:(c + 1) * CHUNK], 0.0) for c in range(n_chunks)], axis=1)
        bot = jnp.concatenate(
            [jnp.where(low_group, 0.0, vj[c * CHUNK:(c + 1) * CHUNK]) for c in range(n_chunks)], axis=1)
        rhs = jnp.concatenate([top, bot], axis=0).astype(BF16)
        mixed = _dot(wcat, rhs)
        bias = jnp.where(low_group,
                         jnp.broadcast_to(bst[:, g0:g0 + 1], (CHUNK, LANES)),
                         jnp.broadcast_to(bst[:, g1:g1 + 1], (CHUNK, LANES)))
        for c in range(n_chunks):
            uj = u[c * CHUNK:(c + 1) * CHUNK, j * LANES:(j + 1) * LANES]
            ya_scr[c * CHUNK:(c + 1) * CHUNK, j * LANES:(j + 1) * LANES] = (
                uj * (mixed[:, c * LANES:(c + 1) * LANES] + bias))
    ya_ref[...] = _rms(ya_scr[...], na_ref[...]).astype(BF16)

    pos_rows = pos_ref[...].astype(F32)
    eye = (lax.broadcasted_iota(jnp.int32, (LANES, LANES), 0)
           == lax.broadcasted_iota(jnp.int32, (LANES, LANES), 1))
    pos_cols = [jnp.sum(jnp.where(eye, pos_rows[a:a + 1, :], 0.0), axis=-1, keepdims=True)
                for a in range(tm // LANES)]
    ang = jnp.concatenate(pos_cols, axis=0) * freq_ref[...]
    cos = jnp.cos(ang)
    sin = jnp.sin(ang)
    lane_t = lax.broadcasted_iota(jnp.int32, (tm, LANES), 1)
    first_half = (lane_t & (HEAD_DIM - 1)) < ROPE_DIM // 2
    sin_signed = jnp.where(first_half, -sin, sin)
    q = proj(2 * D_SGU, D_ATT) * (HEAD_DIM ** -0.5 * LOG2_E)
    q_ref[...] = _rotary(q, cos, sin_signed, first_half)
    k = proj(2 * D_SGU + D_ATT, D_ATT)
    k_ref[...] = _rotary(k, cos, sin_signed, first_half)
    v_ref[...] = proj(2 * D_SGU + 2 * D_ATT, D_ATT)


def _in_proj(x, pos, g, w_in, ln_g, ln_b, w_s, b_st, gmat, freq, norm_a):
    t = x.shape[0]
    d_in = w_in.shape[1]
    half = pl.BlockSpec((TOKEN_TILE, D_SGU), lambda i: (i, 0))
    return pl.pallas_call(
        _in_proj_kernel,
        grid=(t // TOKEN_TILE,),
        in_specs=[pl.BlockSpec((TOKEN_TILE, D_MODEL), lambda i: (i, 0)),
                  pl.BlockSpec((None, TOKEN_TILE // LANES, LANES), lambda i: (i, 0, 0)),
                  _const_spec((1, D_MODEL)), pl.BlockSpec(memory_space=pl.ANY),
                  _const_spec((1, D_SGU)), _const_spec((1, D_SGU)),
                  _const_spec((SGU_GROUPS, CHUNK, CHUNK)), _const_spec((CHUNK, SGU_GROUPS)),
                  _const_spec((D_SGU, D_SGU)), _const_spec((1, LANES)), _const_spec((1, D_SGU))],
        out_specs=[half, half, half, half],
        out_shape=[jax.ShapeDtypeStruct((t, D_SGU), BF16)] + [jax.ShapeDtypeStruct((t, D_ATT), F32)] * 3,
        scratch_shapes=[pltpu.VMEM((TOKEN_TILE, D_SGU), F32), pltpu.VMEM((D_MODEL, d_in), BF16),
                        pltpu.VMEM((2, W_STAGE_ROWS, d_in), F32), pltpu.SemaphoreType.DMA((2,))],
        compiler_params=_params(1),
        name="in_proj",
    )(x, pos, g, w_in, ln_g, ln_b, w_s, b_st, gmat, freq, norm_a)


Q_LO, Q_HI, K_ALL, V_LO, V_HI = range(5)
ACC, DEN, MAX = range(3)
SUB = BAND // 4


def _merge(a, b):
    m = jnp.maximum(a[MAX], b[MAX])
    w_a = jnp.exp2(a[MAX] - m)
    w_b = jnp.exp2(b[MAX] - m)
    return w_a * a[ACC] + w_b * b[ACC], w_a * a[DEN] + w_b * b[DEN], m


def _attn_kernel(q_ref, k_ref, v_ref, y_ref, c4, c16, tmp, st16):
    s_len = q_ref.shape[0]
    n_blk = s_len // BAND
    quarter = s_len // 4
    rows = lambda start, n=BAND: slice(start, start + n)
    lane = lax.broadcasted_iota(jnp.int32, (BAND, LANES), 1)
    lo = lane < HEAD_DIM
    ones_lo = jnp.where(lo, 1.0, 0.0).astype(BF16)
    ones_hi = jnp.where(lo, 0.0, 1.0).astype(BF16)

    def fill(dst, blk, qkv):
        q, k, v = qkv
        dst_rows = rows(blk * BAND)
        dst[Q_LO, dst_rows, :] = jnp.where(lo, q, 0.0).astype(BF16)
        dst[Q_HI, dst_rows, :] = jnp.where(lo, 0.0, q).astype(BF16)
        dst[K_ALL, dst_rows, :] = k.astype(BF16)
        dst[V_LO, dst_rows, :] = jnp.where(lo, v, 0.0).astype(BF16)
        dst[V_HI, dst_rows, :] = jnp.where(lo, 0.0, v).astype(BF16)

    srcs = (q_ref, k_ref, v_ref)
    for r4 in range(4):
        for kb in range(quarter // BAND):
            for i, src in enumerate(srcs):
                tmp[i, rows(r4 * quarter + kb * BAND), :] = src[pl.ds(r4 + 4 * BAND * kb, BAND, stride=4), :]
    for blk in range(n_blk):
        fill(c4, blk, [tmp[i, rows(blk * BAND), :] for i in range(3)])
    for cls in range(n_blk):
        r4, c = cls % 4, cls // 4
        fill(c16, cls, [tmp[i, pl.ds(r4 * quarter + c, BAND, stride=4), :] for i in range(3)])

    qi = lax.broadcasted_iota(jnp.int32, (2 * BAND, BAND), 0) & (BAND - 1)
    kj = lax.broadcasted_iota(jnp.int32, (2 * BAND, BAND), 1)

    def masks(pos):
        neg = jnp.float32(-jnp.inf)
        prev = jnp.where(pos(kj) >= pos(qi), 0.0, neg)
        cur = jnp.where(pos(kj) <= pos(qi), 0.0, neg)
        return {BAND: cur, 2 * BAND: jnp.concatenate([prev, cur], axis=1)}

    masks1 = masks(lambda i: i)
    sub_shift = SUB.bit_length() - 1
    masks4 = masks(lambda i: 4 * (i & (SUB - 1)) + (i >> sub_shift))

    def block(src, q_rows, k_rows, mask):
        gather = lambda idx, rws: jnp.concatenate([src[idx, r, :] for r in rws], axis=0)
        qs = jnp.concatenate([gather(Q_LO, q_rows), gather(Q_HI, q_rows)], axis=0)
        n_k = sum(r.stop - r.start for r in k_rows)
        s = _dot_nt(qs, gather(K_ALL, k_rows)) + mask[n_k]
        m = jnp.max(s, axis=-1, keepdims=True)
        p = jnp.exp2(s - m).astype(BF16)
        p = jnp.concatenate([p[:BAND], p[BAND:]], axis=1)
        reps = n_k // BAND
        rhs = jnp.concatenate(
            [jnp.concatenate([gather(V_LO, k_rows), jnp.concatenate([ones_lo] * reps, axis=0)], axis=1),
             jnp.concatenate([gather(V_HI, k_rows), jnp.concatenate([ones_hi] * reps, axis=0)], axis=1)],
            axis=0)
        r = _dot(p, rhs)
        return r[:, :LANES], r[:, LANES:], jnp.where(lo, m[:BAND], m[BAND:])

    def put(st, dst_rows, state):
        for i in range(3):
            st[i, dst_rows, :] = state[i]

    get = lambda st, r: [st[i, r, :] for i in range(3)]
    part = lambda state, j: [a[j * SUB:(j + 1) * SUB] for a in state]

    for cls in range(n_blk):
        put(st16, rows(cls * BAND), block(c16, [rows(cls * BAND)], [rows(cls * BAND)], masks1))

    for r4 in range(4):
        pieces = lambda kb: [rows((r4 + 4 * c) * BAND + kb * SUB, SUB) for c in range(4)]
        for kb in range(BAND // SUB):
            k_rows = pieces(kb) if kb == 0 else pieces(kb - 1) + pieces(kb)
            state = block(c16, pieces(kb), k_rows, masks4)
            for c, piece in enumerate(pieces(kb)):
                put(tmp, pl.ds(r4 * quarter + kb * BAND + c, SUB, stride=4),
                    _merge(part(state, c), get(st16, piece)))

    for blk in range(n_blk):
        pieces = lambda b: [rows(r4 * quarter + b * SUB, SUB) for r4 in range(4)]
        k_rows = pieces(blk) if blk == 0 else pieces(blk - 1) + pieces(blk)
        state = block(c4, pieces(blk), k_rows, masks4)
        for r4, piece in enumerate(pieces(blk)):
            acc, den, _ = _merge(part(state, r4), get(tmp, piece))
            y_ref[pl.ds(blk * BAND + r4, SUB, stride=4), :] = acc / den


def _attn(q, k, v, batch):
    t = q.shape[0]
    s_len = t // batch
    assert s_len == 16 * BAND and BAND == 4 * SUB
    slab = pl.BlockSpec((s_len, LANES), lambda b, j: (b, j))
    operands = pltpu.VMEM((5, s_len, LANES), BF16)
    state = pltpu.VMEM((3, s_len, LANES), F32)
    return pl.pallas_call(
        _attn_kernel,
        grid=(batch, D_ATT // LANES),
        in_specs=[slab, slab, slab],
        out_specs=slab,
        out_shape=jax.ShapeDtypeStruct((t, D_ATT), F32),
        scratch_shapes=[operands, operands, state, state],
        compiler_params=_params(2),
        name="dilated_attn",
    )(q, k, v)


def _mem_kv_kernel(m_ref, g_ref, wk_hbm, wv_hbm, k_ref, v_ref, wk_ref, wv_ref, stage, sem):
    @pl.when(pl.program_id(0) == 0)
    def _():
        _stream_weight(wk_hbm, wk_ref, stage, sem)
        _stream_weight(wv_hbm, wv_ref, stage, sem)

    shape = m_ref.shape
    mn = _rms(m_ref[...].reshape(-1, D_MODEL), g_ref[...]).astype(BF16)
    k = _dot(mn, wk_ref[...]) * (X_HEAD_DIM ** -0.5)
    k_ref[...] = k.astype(BF16).reshape(shape)
    v_ref[...] = _dot(mn, wv_ref[...]).astype(BF16).reshape(shape)


def _mem_kv(mem, g, wk, wv):
    b = mem.shape[0]
    assert b % MEM_BATCH == 0
    hbm = pl.BlockSpec(memory_space=pl.ANY)
    rows = pl.BlockSpec((MEM_BATCH, N_MEM, D_MODEL), lambda i: (i, 0, 0))
    weight = pltpu.VMEM((D_MODEL, D_MODEL), BF16)
    return pl.pallas_call(
        _mem_kv_kernel,
        grid=(b // MEM_BATCH,),
        in_specs=[rows, _const_spec((1, D_MODEL)), hbm, hbm],
        out_specs=[rows, rows],
        out_shape=[jax.ShapeDtypeStruct((b, N_MEM, D_MODEL), BF16)] * 2,
        scratch_shapes=[weight, weight, pltpu.VMEM((2, W_STAGE_ROWS, D_MODEL), F32),
                        pltpu.SemaphoreType.DMA((2,))],
        compiler_params=_params(1),
        name="mem_kv",
    )(mem, g, wk, wv)


def _cross_kernel(x_ref, ya_ref, yb_ref, nb_ref, wmix_hbm, g_ref, wq_hbm, k_ref, v_ref, wo_hbm, out_ref,
                  wmix_ref, wq_ref, wo_ref, stage, sem):
    @pl.when((pl.program_id(0) == 0) & (pl.program_id(1) == 0))
    def _():
        _stream_weight(wmix_hbm, wmix_ref, stage, sem)
        _stream_weight(wq_hbm, wq_ref, stage, sem)
        _stream_weight(wo_hbm, wo_ref, stage, sem)

    yb = _rms(yb_ref[...], nb_ref[...]).astype(BF16)
    x = (x_ref[...] + _dot(ya_ref[...], wmix_ref[0:D_SGU, :])
         + _dot(yb, wmix_ref[D_SGU:D_SGU + D_ATT, :]))
    h = _rms(x, g_ref[...]).astype(BF16)
    q = _dot(h, wq_ref[...]).astype(BF16)
    heads = []
    for hh in range(X_HEADS):
        cols = slice(hh * X_HEAD_DIM, (hh + 1) * X_HEAD_DIM)
        s = _dot_nt(q[:, cols], k_ref[:, cols])
        p = jnp.exp(s - jnp.max(s, axis=-1, keepdims=True))
        l = jnp.sum(p, axis=-1, keepdims=True)
        heads.append((_dot(p.astype(BF16), v_ref[:, cols]) / l).astype(BF16))
    o = jnp.concatenate(heads, axis=1)
    out_ref[...] = x + _dot(o, wo_ref[...])


def _cross(x, ya, yb, norm_b, w_mix, g, wq, k, v, wo, batch):
    t = x.shape[0]
    per_batch = t // batch // CROSS_TILE
    full = pl.BlockSpec((CROSS_TILE, D_MODEL), lambda b, i: (b * per_batch + i, 0))
    half = pl.BlockSpec((CROSS_TILE, D_ATT), lambda b, i: (b * per_batch + i, 0))
    mem_rows = pl.BlockSpec((None, N_MEM, D_MODEL), lambda b, i: (b, 0, 0))
    hbm = pl.BlockSpec(memory_space=pl.ANY)
    weight = pltpu.VMEM((D_MODEL, D_MODEL), BF16)
    return pl.pallas_call(
        _cross_kernel,
        grid=(batch, per_batch),
        in_specs=[full, half, half, _const_spec((1, D_ATT)), hbm, _const_spec((1, D_MODEL)), hbm,
                  mem_rows, mem_rows, hbm],
        out_specs=full,
        out_shape=jax.ShapeDtypeStruct((t, D_MODEL), F32),
        scratch_shapes=[weight, weight, weight, pltpu.VMEM((2, W_STAGE_ROWS, D_MODEL), F32),
                        pltpu.SemaphoreType.DMA((2,))],
        compiler_params=_params(2),
        name="mix_out_cross",
    )(x, ya, yb, norm_b, w_mix, g, wq, k, v, wo)


def _rope_lane_freq():
    half = ROPE_DIM // 2
    inv_freq = ROPE_THETA ** (-2.0 * jnp.arange(half, dtype=F32) / ROPE_DIM)
    e = np.arange(LANES) % HEAD_DIM
    pick = (e[None, :] < ROPE_DIM) & (e[None, :] % half == np.arange(half)[:, None])
    return jnp.sum(jnp.where(pick, inv_freq[:, None], 0.0), axis=0, keepdims=True)


def _group_mean_matrix():
    g = np.arange(D_SGU) // SGU_GROUP_DIM
    return jnp.asarray(np.where(g[:, None] == g[None, :], 1.0 / SGU_GROUP_DIM, 0.0), dtype=BF16)


def kernel(x, mem, positions, ffn1_norm, ffn1_w_gate, ffn1_w_up, ffn1_w_down, mix_norm, w_in, sgu_ln_g, sgu_ln_b, sgu_w_s, sgu_b_s, out_norm_a, out_norm_b, w_out, cross_norm, mem_norm, cross_wq, cross_wk, cross_wv, cross_wo, ffn2_norm, ffn2_w_gate, ffn2_w_up, ffn2_w_down, final_norm):
    b, s, d = x.shape
    t = b * s
    depth = ffn1_norm.shape[0]
    row = lambda p: p.reshape(1, -1).astype(F32)
    xt = x.reshape(t, d)
    pos = positions.reshape(t // TOKEN_TILE, TOKEN_TILE // LANES, LANES)
    freq = _rope_lane_freq()
    gmat = _group_mean_matrix()
    fin = row(final_norm)
    for l in range(depth):
        xt = _ffn(xt, row(ffn1_norm[l]), ffn1_w_gate[l], ffn1_w_up[l], ffn1_w_down[l], fin,
                  final_norm=False)

        ya, q, k, v = _in_proj(xt, pos, row(mix_norm[l]), w_in[l], row(sgu_ln_g[l]),
                               row(sgu_ln_b[l]), sgu_w_s[l], sgu_b_s[l].T, gmat, freq,
                               row(out_norm_a[l]))
        yb = _attn(q, k, v, b)
        km, vm = _mem_kv(mem, row(mem_norm[l]), cross_wk[l], cross_wv[l])
        xt = _cross(xt, ya, yb, row(out_norm_b[l]), w_out[l], row(cross_norm[l]),
                    cross_wq[l], km, vm, cross_wo[l], b)

        xt = _ffn(xt, row(ffn2_norm[l]), ffn2_w_gate[l], ffn2_w_up[l], ffn2_w_down[l], fin,
                  final_norm=(l == depth - 1))
    return xt.reshape(b, s, d)
```

```python
import functools

import numpy as np
import jax
import jax.numpy as jnp
from jax import lax
from jax.experimental import pallas as pl
from jax.experimental.pallas import tpu as pltpu

F32 = jnp.float32
BF16 = jnp.bfloat16

D_MODEL = 1024
N_MEM = 256
D_SGU = 512
D_ATT = 512
SGU_GROUPS = 8
SGU_GROUP_DIM = D_SGU // SGU_GROUPS
CHUNK = 128
N_HEADS = 8
HEAD_DIM = D_ATT // N_HEADS
BAND = 128
ROPE_THETA = 500000.0
ROPE_DIM = HEAD_DIM // 4
X_HEADS = 4
X_HEAD_DIM = D_MODEL // X_HEADS
D_FF = 2816
NORM_EPS = 1e-6
LN_EPS = 1e-5
LOG2_E = 1.4426950408889634

LANES = 128
TOKEN_TILE = 512
IN_TILE = 1024
CROSS_TILE = 1024
MEM_BATCH = 4
FF_CHUNK = 512
W_STAGE_ROWS = 256
VMEM_LIMIT_BYTES = 56 * 1024 * 1024


def _rms(x, g):
    return x * lax.rsqrt(jnp.mean(x * x, axis=-1, keepdims=True) + NORM_EPS) * g


def _dot(a, b):
    return jnp.dot(a, b, preferred_element_type=F32)


def _dot_nt(a, b):
    return lax.dot_general(a, b, (((1,), (1,)), ((), ())), preferred_element_type=F32)


def _const_spec(shape):
    zeros = (0,) * len(shape)
    return pl.BlockSpec(shape, lambda *_: zeros, pipeline_mode=pl.Buffered(1))


def _stream_weight(w_hbm, w_vmem, stage, sem):
    rows, cols = w_vmem.shape
    chunk = stage.shape[1]
    assert rows % chunk == 0

    def copy(c):
        return pltpu.make_async_copy(w_hbm.at[pl.ds(c * chunk, chunk), :],
                                     stage.at[c % 2, :, pl.ds(0, cols)], sem.at[c % 2])

    copy(0).start()
    for c in range(rows // chunk):
        if c + 1 < rows // chunk:
            copy(c + 1).start()
        copy(c).wait()
        w_vmem[c * chunk:(c + 1) * chunk, :] = stage[c % 2, :, 0:cols].astype(BF16)


def _params(n_axes, **extra):
    return pltpu.CompilerParams(
        dimension_semantics=("arbitrary",) * n_axes,
        vmem_limit_bytes=VMEM_LIMIT_BYTES,
        **extra,
    )


FF_CHUNKS = tuple((c0, min(FF_CHUNK, D_FF - c0)) for c0 in range(0, D_FF, FF_CHUNK))


def _ffn_kernel(x_ref, g_ref, wg_hbm, wu_hbm, wd_hbm, fg_ref, o_ref,
                wg, wu, wd, stage_g, stage_u, stage_d, sem, *, final_norm):
    def chunk_copies(c, slot):
        c0, ck = FF_CHUNKS[c]
        return (
            pltpu.make_async_copy(wg_hbm.at[:, pl.ds(c0, ck)], stage_g.at[slot, :, pl.ds(0, ck)], sem.at[slot, 0]),
            pltpu.make_async_copy(wu_hbm.at[:, pl.ds(c0, ck)], stage_u.at[slot, :, pl.ds(0, ck)], sem.at[slot, 1]),
            pltpu.make_async_copy(wd_hbm.at[pl.ds(c0, ck), :], stage_d.at[slot, pl.ds(0, ck), :], sem.at[slot, 2]),
        )

    def run(load_weights):
        if load_weights:
            for cp in chunk_copies(0, 0):
                cp.start()
        x = x_ref[...]
        h = _rms(x, g_ref[...]).astype(BF16)
        acc = jnp.zeros(x.shape, F32)
        for c, (c0, ck) in enumerate(FF_CHUNKS):
            if load_weights:
                slot = c % 2
                if c + 1 < len(FF_CHUNKS):
                    for cp in chunk_copies(c + 1, 1 - slot):
                        cp.start()
                for cp in chunk_copies(c, slot):
                    cp.wait()
                wg[:, c0:c0 + ck] = stage_g[slot, :, 0:ck].astype(BF16)
                wu[:, c0:c0 + ck] = stage_u[slot, :, 0:ck].astype(BF16)
                wd[c0:c0 + ck, :] = stage_d[slot, 0:ck, :].astype(BF16)
            g = _dot(h, wg[:, c0:c0 + ck])
            u = _dot(h, wu[:, c0:c0 + ck])
            a = (g * jax.nn.sigmoid(g)) * u
            acc = acc + _dot(a.astype(BF16), wd[c0:c0 + ck, :])
        y = x + 0.5 * acc
        if final_norm:
            y = _rms(y, fg_ref[...])
        o_ref[...] = y

    first = pl.program_id(0) == 0
    pl.when(first)(functools.partial(run, True))
    pl.when(jnp.logical_not(first))(functools.partial(run, False))


def _ffn(x, g, wg, wu, wd, fg, *, final_norm):
    t = x.shape[0]
    tile = pl.BlockSpec((TOKEN_TILE, D_MODEL), lambda i: (i, 0))
    hbm = pl.BlockSpec(memory_space=pl.ANY)
    return pl.pallas_call(
        functools.partial(_ffn_kernel, final_norm=final_norm),
        grid=(t // TOKEN_TILE,),
        in_specs=[tile, _const_spec((1, D_MODEL)), hbm, hbm, hbm, _const_spec((1, D_MODEL))],
        out_specs=tile,
        out_shape=jax.ShapeDtypeStruct((t, D_MODEL), F32),
        scratch_shapes=[pltpu.VMEM((D_MODEL, D_FF), BF16), pltpu.VMEM((D_MODEL, D_FF), BF16),
                        pltpu.VMEM((D_FF, D_MODEL), BF16),
                        pltpu.VMEM((2, D_MODEL, FF_CHUNK), F32), pltpu.VMEM((2, D_MODEL, FF_CHUNK), F32),
                        pltpu.VMEM((2, FF_CHUNK, D_MODEL), F32),
                        pltpu.SemaphoreType.DMA((2, 3))],
        compiler_params=_params(1),
        name="ffn_final" if final_norm else "ffn",
    )(x, g, wg, wu, wd, fg)


def _gelu(x):
    return 0.5 * x * (1.0 + lax.erf(x * (2.0 ** -0.5)))


def _group_mean(x, gmat):
    return _dot(x.astype(BF16), gmat)


def _rotary(t, cos, sin_signed, first_half):
    parts = []
    for j in range(D_ATT // LANES):
        tj = t[:, j * LANES:(j + 1) * LANES]
        partner = jnp.where(first_half,
                            pltpu.roll(tj, LANES - ROPE_DIM // 2, axis=1),
                            pltpu.roll(tj, ROPE_DIM // 2, axis=1))
        parts.append(tj * cos + partner * sin_signed)
    return jnp.concatenate(parts, axis=1)


def _in_proj_kernel(x_ref, pos_ref, g_ref, win_hbm, lng_ref, lnb_ref, ws_ref, bst_ref,
                    gmat_ref, freq_ref, na_ref, ya_ref, q_ref, k_ref, v_ref,
                    ya_scr, win_ref, stage, sem):
    @pl.when(pl.program_id(0) == 0)
    def _():
        _stream_weight(win_hbm, win_ref, stage, sem)

    tm = x_ref.shape[0]
    n_chunks = tm // CHUNK
    h = _rms(x_ref[...], g_ref[...]).astype(BF16)

    def proj(c0, width):
        return _dot(h, win_ref[:, c0:c0 + width])

    u = _gelu(proj(0, D_SGU))
    v = _gelu(proj(D_SGU, D_SGU))
    gmat = gmat_ref[...]
    d = v - _group_mean(v, gmat)
    var = _group_mean(d * d, gmat)
    vln = d * lax.rsqrt(var + LN_EPS) * lng_ref[...] + lnb_ref[...]

    lane = lax.broadcasted_iota(jnp.int32, (CHUNK, LANES), 1)
    low_group = lane < SGU_GROUP_DIM
    row = lax.broadcasted_iota(jnp.int32, (CHUNK, CHUNK), 0)
    col = lax.broadcasted_iota(jnp.int32, (CHUNK, CHUNK), 1)
    causal = col <= row
    bst = bst_ref[...]
    for j in range(D_SGU // LANES):
        g0, g1 = 2 * j, 2 * j + 1
        w0 = jnp.where(causal, ws_ref[g0], 0.0)
        w1 = jnp.where(causal, ws_ref[g1], 0.0)
        wcat = jnp.concatenate([w0, w1], axis=1).astype(BF16)
        vj = vln[:, j * LANES:(j + 1) * LANES]
        top = jnp.concatenate(
            [jnp.where(low_group, vj[c * CHUNK:(c + 1) * CHUNK], 0.0) for c in range(n_chunks)], axis=1)
        bot = jnp.concatenate(
            [jnp.where(low_group, 0.0, vj[c * CHUNK:(c + 1) * CHUNK]) for c in range(n_chunks)], axis=1)
        rhs = jnp.concatenate([top, bot], axis=0).astype(BF16)
        mixed = _dot(wcat, rhs)
        bias = jnp.where(low_group,
                         jnp.broadcast_to(bst[:, g0:g0 + 1], (CHUNK, LANES)),
                         jnp.broadcast_to(bst[:, g1:g1 + 1], (CHUNK, LANES)))
        for c in range(n_chunks):
            uj = u[c * CHUNK:(c + 1) * CHUNK, j * LANES:(j + 1) * LANES]
            ya_scr[c * CHUNK:(c + 1) * CHUNK, j * LANES:(j + 1) * LANES] = (
                uj * (mixed[:, c * LANES:(c + 1) * LANES] + bias))
    ya_ref[...] = _rms(ya_scr[...], na_ref[...]).astype(BF16)

    pos_rows = pos_ref[...].astype(F32)
    eye = (lax.broadcasted_iota(jnp.int32, (LANES, LANES), 0)
           == lax.broadcasted_iota(jnp.int32, (LANES, LANES), 1))
    pos_cols = [jnp.sum(jnp.where(eye, pos_rows[a:a + 1, :], 0.0), axis=-1, keepdims=True)
                for a in range(tm // LANES)]
    ang = jnp.concatenate(pos_cols, axis=0) * freq_ref[...]
    cos = jnp.cos(ang)
    sin = jnp.sin(ang)
    lane_t = lax.broadcasted_iota(jnp.int32, (tm, LANES), 1)
    first_half = (lane_t & (HEAD_DIM - 1)) < ROPE_DIM // 2
    sin_signed = jnp.where(first_half, -sin, sin)
    q = proj(2 * D_SGU, D_ATT) * (HEAD_DIM ** -0.5 * LOG2_E)
    q_ref[...] = _rotary(q, cos, sin_signed, first_half)
    k = proj(2 * D_SGU + D_ATT, D_ATT)
    k_ref[...] = _rotary(k, cos, sin_signed, first_half)
    v_ref[...] = proj(2 * D_SGU + 2 * D_ATT, D_ATT)


def _in_proj(x, pos, g, w_in, ln_g, ln_b, w_s, b_st, gmat, freq, norm_a):
    t = x.shape[0]
    d_in = w_in.shape[1]
    half = pl.BlockSpec((IN_TILE, D_SGU), lambda i: (i, 0))
    return pl.pallas_call(
        _in_proj_kernel,
        grid=(t // IN_TILE,),
        in_specs=[pl.BlockSpec((IN_TILE, D_MODEL), lambda i: (i, 0)),
                  pl.BlockSpec((None, IN_TILE // LANES, LANES), lambda i: (i, 0, 0)),
                  _const_spec((1, D_MODEL)), pl.BlockSpec(memory_space=pl.ANY),
                  _const_spec((1, D_SGU)), _const_spec((1, D_SGU)),
                  _const_spec((SGU_GROUPS, CHUNK, CHUNK)), _const_spec((CHUNK, SGU_GROUPS)),
                  _const_spec((D_SGU, D_SGU)), _const_spec((1, LANES)), _const_spec((1, D_SGU))],
        out_specs=[half, half, half, half],
        out_shape=[jax.ShapeDtypeStruct((t, D_SGU), BF16)] + [jax.ShapeDtypeStruct((t, D_ATT), F32)] * 3,
        scratch_shapes=[pltpu.VMEM((IN_TILE, D_SGU), F32), pltpu.VMEM((D_MODEL, d_in), BF16),
                        pltpu.VMEM((2, W_STAGE_ROWS, d_in), F32), pltpu.SemaphoreType.DMA((2,))],
        compiler_params=_params(1),
        name="in_proj",
    )(x, pos, g, w_in, ln_g, ln_b, w_s, b_st, gmat, freq, norm_a)


Q_LO, Q_HI, K_ALL, V_LO, V_HI = range(5)
ACC, DEN, MAX = range(3)
SUB = BAND // 4


def _merge(a, b):
    m = jnp.maximum(a[MAX], b[MAX])
    w_a = jnp.exp2(a[MAX] - m)
    w_b = jnp.exp2(b[MAX] - m)
    return w_a * a[ACC] + w_b * b[ACC], w_a * a[DEN] + w_b * b[DEN], m


def _attn_kernel(q_ref, k_ref, v_ref, y_ref, c4, c16, tmp, st16):
    s_len = q_ref.shape[0]
    n_blk = s_len // BAND
    quarter = s_len // 4
    rows = lambda start, n=BAND: slice(start, start + n)
    lane = lax.broadcasted_iota(jnp.int32, (BAND, LANES), 1)
    lo = lane < HEAD_DIM
    ones_lo = jnp.where(lo, 1.0, 0.0).astype(BF16)
    ones_hi = jnp.where(lo, 0.0, 1.0).astype(BF16)

    def fill(dst, blk, qkv):
        q, k, v = qkv
        dst_rows = rows(blk * BAND)
        dst[Q_LO, dst_rows, :] = jnp.where(lo, q, 0.0).astype(BF16)
        dst[Q_HI, dst_rows, :] = jnp.where(lo, 0.0, q).astype(BF16)
        dst[K_ALL, dst_rows, :] = k.astype(BF16)
        dst[V_LO, dst_rows, :] = jnp.where(lo, v, 0.0).astype(BF16)
        dst[V_HI, dst_rows, :] = jnp.where(lo, 0.0, v).astype(BF16)

    srcs = (q_ref, k_ref, v_ref)
    for r4 in range(4):
        for kb in range(quarter // BAND):
            for i, src in enumerate(srcs):
                tmp[i, rows(r4 * quarter + kb * BAND), :] = src[pl.ds(r4 + 4 * BAND * kb, BAND, stride=4), :]
    for blk in range(n_blk):
        fill(c4, blk, [tmp[i, rows(blk * BAND), :] for i in range(3)])
    for cls in range(n_blk):
        r4, c = cls % 4, cls // 4
        fill(c16, cls, [tmp[i, pl.ds(r4 * quarter + c, BAND, stride=4), :] for i in range(3)])

    qi = lax.broadcasted_iota(jnp.int32, (2 * BAND, BAND), 0) & (BAND - 1)
    kj = lax.broadcasted_iota(jnp.int32, (2 * BAND, BAND), 1)

    def masks(pos):
        neg = jnp.float32(-jnp.inf)
        prev = jnp.where(pos(kj) >= pos(qi), 0.0, neg)
        cur = jnp.where(pos(kj) <= pos(qi), 0.0, neg)
        return {BAND: cur, 2 * BAND: jnp.concatenate([prev, cur], axis=1)}

    masks1 = masks(lambda i: i)
    sub_shift = SUB.bit_length() - 1
    masks4 = masks(lambda i: 4 * (i & (SUB - 1)) + (i >> sub_shift))

    def block(src, q_rows, k_rows, mask):
        gather = lambda idx, rws: jnp.concatenate([src[idx, r, :] for r in rws], axis=0)
        qs = jnp.concatenate([gather(Q_LO, q_rows), gather(Q_HI, q_rows)], axis=0)
        n_k = sum(r.stop - r.start for r in k_rows)
        s = _dot_nt(qs, gather(K_ALL, k_rows)) + mask[n_k]
        m = jnp.max(s, axis=-1, keepdims=True)
        p = jnp.exp2(s - m).astype(BF16)
        p = jnp.concatenate([p[:BAND], p[BAND:]], axis=1)
        reps = n_k // BAND
        rhs = jnp.concatenate(
            [jnp.concatenate([gather(V_LO, k_rows), jnp.concatenate([ones_lo] * reps, axis=0)], axis=1),
             jnp.concatenate([gather(V_HI, k_rows), jnp.concatenate([ones_hi] * reps, axis=0)], axis=1)],
            axis=0)
        r = _dot(p, rhs)
        return r[:, :LANES], r[:, LANES:], jnp.where(lo, m[:BAND], m[BAND:])

    def put(st, dst_rows, state):
        for i in range(3):
            st[i, dst_rows, :] = state[i]

    get = lambda st, r: [st[i, r, :] for i in range(3)]
    part = lambda state, j: [a[j * SUB:(j + 1) * SUB] for a in state]

    for cls in range(n_blk):
        put(st16, rows(cls * BAND), block(c16, [rows(cls * BAND)], [rows(cls * BAND)], masks1))

    for r4 in range(4):
        pieces = lambda kb: [rows((r4 + 4 * c) * BAND + kb * SUB, SUB) for c in range(4)]
        for kb in range(BAND // SUB):
            k_rows = pieces(kb) if kb == 0 else pieces(kb - 1) + pieces(kb)
            state = block(c16, pieces(kb), k_rows, masks4)
            for c, piece in enumerate(pieces(kb)):
                put(tmp, pl.ds(r4 * quarter + kb * BAND + c, SUB, stride=4),
                    _merge(part(state, c), get(st16, piece)))

    for blk in range(n_blk):
        pieces = lambda b: [rows(r4 * quarter + b * SUB, SUB) for r4 in range(4)]
        k_rows = pieces(blk) if blk == 0 else pieces(blk - 1) + pieces(blk)
        state = block(c4, pieces(blk), k_rows, masks4)
        for r4, piece in enumerate(pieces(blk)):
            acc, den, _ = _merge(part(state, r4), get(tmp, piece))
            y_ref[pl.ds(blk * BAND + r4, SUB, stride=4), :] = acc / den


def _attn(q, k, v, batch):
    t = q.shape[0]
    s_len = t // batch
    assert s_len == 16 * BAND and BAND == 4 * SUB
    slab = pl.BlockSpec((s_len, LANES), lambda b, j: (b, j))
    operands = pltpu.VMEM((5, s_len, LANES), BF16)
    state = pltpu.VMEM((3, s_len, LANES), F32)
    return pl.pallas_call(
        _attn_kernel,
        grid=(batch, D_ATT // LANES),
        in_specs=[slab, slab, slab],
        out_specs=slab,
        out_shape=jax.ShapeDtypeStruct((t, D_ATT), F32),
        scratch_shapes=[operands, operands, state, state],
        compiler_params=_params(2),
        name="dilated_attn",
    )(q, k, v)


def _mem_kv_kernel(m_ref, g_ref, wk_hbm, wv_hbm, k_ref, v_ref, wk_ref, wv_ref, stage, sem):
    @pl.when(pl.program_id(0) == 0)
    def _():
        _stream_weight(wk_hbm, wk_ref, stage, sem)
        _stream_weight(wv_hbm, wv_ref, stage, sem)

    shape = m_ref.shape
    mn = _rms(m_ref[...].reshape(-1, D_MODEL), g_ref[...]).astype(BF16)
    k = _dot(mn, wk_ref[...]) * (X_HEAD_DIM ** -0.5)
    k_ref[...] = k.astype(BF16).reshape(shape)
    v_ref[...] = _dot(mn, wv_ref[...]).astype(BF16).reshape(shape)


def _mem_kv(mem, g, wk, wv):
    b = mem.shape[0]
    assert b % MEM_BATCH == 0
    hbm = pl.BlockSpec(memory_space=pl.ANY)
    rows = pl.BlockSpec((MEM_BATCH, N_MEM, D_MODEL), lambda i: (i, 0, 0))
    weight = pltpu.VMEM((D_MODEL, D_MODEL), BF16)
    return pl.pallas_call(
        _mem_kv_kernel,
        grid=(b // MEM_BATCH,),
        in_specs=[rows, _const_spec((1, D_MODEL)), hbm, hbm],
        out_specs=[rows, rows],
        out_shape=[jax.ShapeDtypeStruct((b, N_MEM, D_MODEL), BF16)] * 2,
        scratch_shapes=[weight, weight, pltpu.VMEM((2, W_STAGE_ROWS, D_MODEL), F32),
                        pltpu.SemaphoreType.DMA((2,))],
        compiler_params=_params(1),
        name="mem_kv",
    )(mem, g, wk, wv)


def _cross_kernel(x_ref, ya_ref, yb_ref, nb_ref, wmix_hbm, g_ref, wq_hbm, k_ref, v_ref, wo_hbm, out_ref,
                  wmix_ref, wq_ref, wo_ref, stage, sem):
    @pl.when((pl.program_id(0) == 0) & (pl.program_id(1) == 0))
    def _():
        _stream_weight(wmix_hbm, wmix_ref, stage, sem)
        _stream_weight(wq_hbm, wq_ref, stage, sem)
        _stream_weight(wo_hbm, wo_ref, stage, sem)

    yb = _rms(yb_ref[...], nb_ref[...]).astype(BF16)
    x = (x_ref[...] + _dot(ya_ref[...], wmix_ref[0:D_SGU, :])
         + _dot(yb, wmix_ref[D_SGU:D_SGU + D_ATT, :]))
    h = _rms(x, g_ref[...]).astype(BF16)
    q = _dot(h, wq_ref[...]).astype(BF16)
    heads = []
    for hh in range(X_HEADS):
        cols = slice(hh * X_HEAD_DIM, (hh + 1) * X_HEAD_DIM)
        s = _dot_nt(q[:, cols], k_ref[:, cols])
        p = jnp.exp(s - jnp.max(s, axis=-1, keepdims=True))
        l = jnp.sum(p, axis=-1, keepdims=True)
        heads.append((_dot(p.astype(BF16), v_ref[:, cols]) / l).astype(BF16))
    o = jnp.concatenate(heads, axis=1)
    out_ref[...] = x + _dot(o, wo_ref[...])


def _cross(x, ya, yb, norm_b, w_mix, g, wq, k, v, wo, batch):
    t = x.shape[0]
    per_batch = t // batch // CROSS_TILE
    full = pl.BlockSpec((CROSS_TILE, D_MODEL), lambda b, i: (b * per_batch + i, 0))
    half = pl.BlockSpec((CROSS_TILE, D_ATT), lambda b, i: (b * per_batch + i, 0))
    mem_rows = pl.BlockSpec((None, N_MEM, D_MODEL), lambda b, i: (b, 0, 0))
    hbm = pl.BlockSpec(memory_space=pl.ANY)
    weight = pltpu.VMEM((D_MODEL, D_MODEL), BF16)
    return pl.pallas_call(
        _cross_kernel,
        grid=(batch, per_batch),
        in_specs=[full, half, half, _const_spec((1, D_ATT)), hbm, _const_spec((1, D_MODEL)), hbm,
                  mem_rows, mem_rows, hbm],
        out_specs=full,
        out_shape=jax.ShapeDtypeStruct((t, D_MODEL), F32),
        scratch_shapes=[weight, weight, weight, pltpu.VMEM((2, W_STAGE_ROWS, D_MODEL), F32),
                        pltpu.SemaphoreType.DMA((2,))],
        compiler_params=_params(2),
        name="mix_out_cross",
    )(x, ya, yb, norm_b, w_mix, g, wq, k, v, wo)


def _rope_lane_freq():
    half = ROPE_DIM // 2
    inv_freq = ROPE_THETA ** (-2.0 * jnp.arange(half, dtype=F32) / ROPE_DIM)
    e = np.arange(LANES) % HEAD_DIM
    pick = (e[None, :] < ROPE_DIM) & (e[None, :] % half == np.arange(half)[:, None])
    return jnp.sum(jnp.where(pick, inv_freq[:, None], 0.0), axis=0, keepdims=True)


def _group_mean_matrix():
    g = np.arange(D_SGU) // SGU_GROUP_DIM
    return jnp.asarray(np.where(g[:, None] == g[None, :], 1.0 / SGU_GROUP_DIM, 0.0), dtype=BF16)


def kernel(x, mem, positions, ffn1_norm, ffn1_w_gate, ffn1_w_up, ffn1_w_down, mix_norm, w_in, sgu_ln_g, sgu_ln_b, sgu_w_s, sgu_b_s, out_norm_a, out_norm_b, w_out, cross_norm, mem_norm, cross_wq, cross_wk, cross_wv, cross_wo, ffn2_norm, ffn2_w_gate, ffn2_w_up, ffn2_w_down, final_norm):
    b, s, d = x.shape
    t = b * s
    depth = ffn1_norm.shape[0]
    row = lambda p: p.reshape(1, -1).astype(F32)
    xt = x.reshape(t, d)
    pos = positions.reshape(t // IN_TILE, IN_TILE // LANES, LANES)
    freq = _rope_lane_freq()
    gmat = _group_mean_matrix()
    fin = row(final_norm)
    for l in range(depth):
        xt = _ffn(xt, row(ffn1_norm[l]), ffn1_w_gate[l], ffn1_w_up[l], ffn1_w_down[l], fin,
                  final_norm=False)

        ya, q, k, v = _in_proj(xt, pos, row(mix_norm[l]), w_in[l], row(sgu_ln_g[l]),
                               row(sgu_ln_b[l]), sgu_w_s[l], sgu_b_s[l].T, gmat, freq,
                               row(out_norm_a[l]))
        yb = _attn(q, k, v, b)
        km, vm = _mem_kv(mem, row(mem_norm[l]), cross_wk[l], cross_wv[l])
        xt = _cross(xt, ya, yb, row(out_norm_b[l]), w_out[l], row(cross_norm[l]),
                    cross_wq[l], km, vm, cross_wo[l], b)

        xt = _ffn(xt, row(ffn2_norm[l]), ffn2_w_gate[l], ffn2_w_up[l], ffn2_w_down[l], fin,
                  final_norm=(l == depth - 1))
    return xt.reshape(b, s, d)
```

```python
import functools

import numpy as np
import jax
import jax.numpy as jnp
from jax import lax
from jax.experimental import pallas as pl
from jax.experimental.pallas import tpu as pltpu

F32 = jnp.float32
BF16 = jnp.bfloat16

D_MODEL = 1024
N_MEM = 256
D_SGU = 512
D_ATT = 512
SGU_GROUPS = 8
SGU_GROUP_DIM = D_SGU // SGU_GROUPS
CHUNK = 128
N_HEADS = 8
HEAD_DIM = D_ATT // N_HEADS
BAND = 128
ROPE_THETA = 500000.0
ROPE_DIM = HEAD_DIM // 4
X_HEADS = 4
X_HEAD_DIM = D_MODEL // X_HEADS
D_FF = 2816
NORM_EPS = 1e-6
LN_EPS = 1e-5
LOG2_E = 1.4426950408889634

LANES = 128
TOKEN_TILE = 512
IN_TILE = 1024
CROSS_TILE = 1024
FF_CHUNK = 512
W_STAGE_ROWS = 256
VMEM_LIMIT_BYTES = 56 * 1024 * 1024


def _rms(x, g):
    return x * lax.rsqrt(jnp.mean(x * x, axis=-1, keepdims=True) + NORM_EPS) * g


def _dot(a, b):
    return jnp.dot(a, b, preferred_element_type=F32)


def _dot_nt(a, b):
    return lax.dot_general(a, b, (((1,), (1,)), ((), ())), preferred_element_type=F32)


def _const_spec(shape):
    zeros = (0,) * len(shape)
    return pl.BlockSpec(shape, lambda *_: zeros, pipeline_mode=pl.Buffered(1))


def _stream_weight(w_hbm, w_vmem, stage, sem):
    rows, cols = w_vmem.shape
    chunk = stage.shape[1]
    assert rows % chunk == 0

    def copy(c):
        return pltpu.make_async_copy(w_hbm.at[pl.ds(c * chunk, chunk), :],
                                     stage.at[c % 2, :, pl.ds(0, cols)], sem.at[c % 2])

    copy(0).start()
    for c in range(rows // chunk):
        if c + 1 < rows // chunk:
            copy(c + 1).start()
        copy(c).wait()
        w_vmem[c * chunk:(c + 1) * chunk, :] = stage[c % 2, :, 0:cols].astype(BF16)


def _params(n_axes, **extra):
    return pltpu.CompilerParams(
        dimension_semantics=("arbitrary",) * n_axes,
        vmem_limit_bytes=VMEM_LIMIT_BYTES,
        **extra,
    )


FF_CHUNKS = tuple((c0, min(FF_CHUNK, D_FF - c0)) for c0 in range(0, D_FF, FF_CHUNK))


def _ffn_kernel(x_ref, g_ref, wg_hbm, wu_hbm, wd_hbm, fg_ref, o_ref,
                wg, wu, wd, stage_g, stage_u, stage_d, sem, *, final_norm):
    def chunk_copies(c, slot):
        c0, ck = FF_CHUNKS[c]
        return (
            pltpu.make_async_copy(wg_hbm.at[:, pl.ds(c0, ck)], stage_g.at[slot, :, pl.ds(0, ck)], sem.at[slot, 0]),
            pltpu.make_async_copy(wu_hbm.at[:, pl.ds(c0, ck)], stage_u.at[slot, :, pl.ds(0, ck)], sem.at[slot, 1]),
            pltpu.make_async_copy(wd_hbm.at[pl.ds(c0, ck), :], stage_d.at[slot, pl.ds(0, ck), :], sem.at[slot, 2]),
        )

    def run(load_weights):
        if load_weights:
            for cp in chunk_copies(0, 0):
                cp.start()
        x = x_ref[...]
        h = _rms(x, g_ref[...]).astype(BF16)
        acc = jnp.zeros(x.shape, F32)
        for c, (c0, ck) in enumerate(FF_CHUNKS):
            if load_weights:
                slot = c % 2
                if c + 1 < len(FF_CHUNKS):
                    for cp in chunk_copies(c + 1, 1 - slot):
                        cp.start()
                for cp in chunk_copies(c, slot):
                    cp.wait()
                wg[:, c0:c0 + ck] = stage_g[slot, :, 0:ck].astype(BF16)
                wu[:, c0:c0 + ck] = stage_u[slot, :, 0:ck].astype(BF16)
                wd[c0:c0 + ck, :] = stage_d[slot, 0:ck, :].astype(BF16)
            g = _dot(h, wg[:, c0:c0 + ck])
            u = _dot(h, wu[:, c0:c0 + ck])
            a = (g * jax.nn.sigmoid(g)) * u
            acc = acc + _dot(a.astype(BF16), wd[c0:c0 + ck, :])
        y = x + 0.5 * acc
        if final_norm:
            y = _rms(y, fg_ref[...])
        o_ref[...] = y

    first = pl.program_id(0) == 0
    pl.when(first)(functools.partial(run, True))
    pl.when(jnp.logical_not(first))(functools.partial(run, False))


def _ffn(x, g, wg, wu, wd, fg, *, final_norm):
    t = x.shape[0]
    tile = pl.BlockSpec((TOKEN_TILE, D_MODEL), lambda i: (i, 0))
    hbm = pl.BlockSpec(memory_space=pl.ANY)
    return pl.pallas_call(
        functools.partial(_ffn_kernel, final_norm=final_norm),
        grid=(t // TOKEN_TILE,),
        in_specs=[tile, _const_spec((1, D_MODEL)), hbm, hbm, hbm, _const_spec((1, D_MODEL))],
        out_specs=tile,
        out_shape=jax.ShapeDtypeStruct((t, D_MODEL), F32),
        scratch_shapes=[pltpu.VMEM((D_MODEL, D_FF), BF16), pltpu.VMEM((D_MODEL, D_FF), BF16),
                        pltpu.VMEM((D_FF, D_MODEL), BF16),
                        pltpu.VMEM((2, D_MODEL, FF_CHUNK), F32), pltpu.VMEM((2, D_MODEL, FF_CHUNK), F32),
                        pltpu.VMEM((2, FF_CHUNK, D_MODEL), F32),
                        pltpu.SemaphoreType.DMA((2, 3))],
        compiler_params=_params(1),
        name="ffn_final" if final_norm else "ffn",
    )(x, g, wg, wu, wd, fg)


def _gelu(x):
    return 0.5 * x * (1.0 + lax.erf(x * (2.0 ** -0.5)))


def _group_mean(x, gmat):
    return _dot(x.astype(BF16), gmat)


def _rotary(t, cos, sin_signed, first_half):
    parts = []
    for j in range(D_ATT // LANES):
        tj = t[:, j * LANES:(j + 1) * LANES]
        partner = jnp.where(first_half,
                            pltpu.roll(tj, LANES - ROPE_DIM // 2, axis=1),
                            pltpu.roll(tj, ROPE_DIM // 2, axis=1))
        parts.append(tj * cos + partner * sin_signed)
    return jnp.concatenate(parts, axis=1)


def _in_proj_kernel(x_ref, pos_ref, g_ref, win_hbm, lng_ref, lnb_ref, ws_ref, bst_ref,
                    gmat_ref, freq_ref, na_ref, ya_ref, q_ref, k_ref, v_ref,
                    ya_scr, win_ref, stage, sem):
    @pl.when(pl.program_id(0) == 0)
    def _():
        _stream_weight(win_hbm, win_ref, stage, sem)

    tm = x_ref.shape[0]
    n_chunks = tm // CHUNK
    h = _rms(x_ref[...], g_ref[...]).astype(BF16)

    def proj(c0, width):
        return _dot(h, win_ref[:, c0:c0 + width])

    u = _gelu(proj(0, D_SGU))
    v = _gelu(proj(D_SGU, D_SGU))
    gmat = gmat_ref[...]
    d = v - _group_mean(v, gmat)
    var = _group_mean(d * d, gmat)
    vln = d * lax.rsqrt(var + LN_EPS) * lng_ref[...] + lnb_ref[...]

    lane = lax.broadcasted_iota(jnp.int32, (CHUNK, LANES), 1)
    low_group = lane < SGU_GROUP_DIM
    row = lax.broadcasted_iota(jnp.int32, (CHUNK, CHUNK), 0)
    col = lax.broadcasted_iota(jnp.int32, (CHUNK, CHUNK), 1)
    causal = col <= row
    bst = bst_ref[...]
    for j in range(D_SGU // LANES):
        g0, g1 = 2 * j, 2 * j + 1
        w0 = jnp.where(causal, ws_ref[g0], 0.0)
        w1 = jnp.where(causal, ws_ref[g1], 0.0)
        wcat = jnp.concatenate([w0, w1], axis=1).astype(BF16)
        vj = vln[:, j * LANES:(j + 1) * LANES]
        top = jnp.concatenate(
            [jnp.where(low_group, vj[c * CHUNK:(c + 1) * CHUNK], 0.0) for c in range(n_chunks)], axis=1)
        bot = jnp.concatenate(
            [jnp.where(low_group, 0.0, vj[c * CHUNK:(c + 1) * CHUNK]) for c in range(n_chunks)], axis=1)
        rhs = jnp.concatenate([top, bot], axis=0).astype(BF16)
        mixed = _dot(wcat, rhs)
        bias = jnp.where(low_group,
                         jnp.broadcast_to(bst[:, g0:g0 + 1], (CHUNK, LANES)),
                         jnp.broadcast_to(bst[:, g1:g1 + 1], (CHUNK, LANES)))
        for c in range(n_chunks):
            uj = u[c * CHUNK:(c + 1) * CHUNK, j * LANES:(j + 1) * LANES]
            ya_scr[c * CHUNK:(c + 1) * CHUNK, j * LANES:(j + 1) * LANES] = (
                uj * (mixed[:, c * LANES:(c + 1) * LANES] + bias))
    ya_ref[...] = _rms(ya_scr[...], na_ref[...]).astype(BF16)

    pos_rows = pos_ref[...].astype(F32)
    eye = (lax.broadcasted_iota(jnp.int32, (LANES, LANES), 0)
           == lax.broadcasted_iota(jnp.int32, (LANES, LANES), 1))
    pos_cols = [jnp.sum(jnp.where(eye, pos_rows[a:a + 1, :], 0.0), axis=-1, keepdims=True)
                for a in range(tm // LANES)]
    ang = jnp.concatenate(pos_cols, axis=0) * freq_ref[...]
    cos = jnp.cos(ang)
    sin = jnp.sin(ang)
    lane_t = lax.broadcasted_iota(jnp.int32, (tm, LANES), 1)
    first_half = (lane_t & (HEAD_DIM - 1)) < ROPE_DIM // 2
    sin_signed = jnp.where(first_half, -sin, sin)
    q = proj(2 * D_SGU, D_ATT) * (HEAD_DIM ** -0.5 * LOG2_E)
    q_ref[...] = _rotary(q, cos, sin_signed, first_half)
    k = proj(2 * D_SGU + D_ATT, D_ATT)
    k_ref[...] = _rotary(k, cos, sin_signed, first_half)
    v_ref[...] = proj(2 * D_SGU + 2 * D_ATT, D_ATT)


def _in_proj(x, pos, g, w_in, ln_g, ln_b, w_s, b_st, gmat, freq, norm_a):
    t = x.shape[0]
    d_in = w_in.shape[1]
    half = pl.BlockSpec((IN_TILE, D_SGU), lambda i: (i, 0))
    return pl.pallas_call(
        _in_proj_kernel,
        grid=(t // IN_TILE,),
        in_specs=[pl.BlockSpec((IN_TILE, D_MODEL), lambda i: (i, 0)),
                  pl.BlockSpec((None, IN_TILE // LANES, LANES), lambda i: (i, 0, 0)),
                  _const_spec((1, D_MODEL)), pl.BlockSpec(memory_space=pl.ANY),
                  _const_spec((1, D_SGU)), _const_spec((1, D_SGU)),
                  _const_spec((SGU_GROUPS, CHUNK, CHUNK)), _const_spec((CHUNK, SGU_GROUPS)),
                  _const_spec((D_SGU, D_SGU)), _const_spec((1, LANES)), _const_spec((1, D_SGU))],
        out_specs=[half, half, half, half],
        out_shape=[jax.ShapeDtypeStruct((t, D_SGU), BF16)] + [jax.ShapeDtypeStruct((t, D_ATT), F32)] * 3,
        scratch_shapes=[pltpu.VMEM((IN_TILE, D_SGU), F32), pltpu.VMEM((D_MODEL, d_in), BF16),
                        pltpu.VMEM((2, W_STAGE_ROWS, d_in), F32), pltpu.SemaphoreType.DMA((2,))],
        compiler_params=_params(1),
        name="in_proj",
    )(x, pos, g, w_in, ln_g, ln_b, w_s, b_st, gmat, freq, norm_a)


Q_LO, Q_HI, K_ALL, V_LO, V_HI = range(5)
ACC, DEN, MAX = range(3)
SUB = BAND // 4


def _merge(a, b):
    m = jnp.maximum(a[MAX], b[MAX])
    w_a = jnp.exp2(a[MAX] - m)
    w_b = jnp.exp2(b[MAX] - m)
    return w_a * a[ACC] + w_b * b[ACC], w_a * a[DEN] + w_b * b[DEN], m


def _attn_kernel(q_ref, k_ref, v_ref, y_ref, c4, c16, tmp, st16):
    s_len = q_ref.shape[0]
    n_blk = s_len // BAND
    quarter = s_len // 4
    rows = lambda start, n=BAND: slice(start, start + n)
    lane = lax.broadcasted_iota(jnp.int32, (BAND, LANES), 1)
    lo = lane < HEAD_DIM
    ones_lo = jnp.where(lo, 1.0, 0.0).astype(BF16)
    ones_hi = jnp.where(lo, 0.0, 1.0).astype(BF16)

    def fill(dst, blk, qkv):
        q, k, v = qkv
        dst_rows = rows(blk * BAND)
        dst[Q_LO, dst_rows, :] = jnp.where(lo, q, 0.0).astype(BF16)
        dst[Q_HI, dst_rows, :] = jnp.where(lo, 0.0, q).astype(BF16)
        dst[K_ALL, dst_rows, :] = k.astype(BF16)
        dst[V_LO, dst_rows, :] = jnp.where(lo, v, 0.0).astype(BF16)
        dst[V_HI, dst_rows, :] = jnp.where(lo, 0.0, v).astype(BF16)

    srcs = (q_ref, k_ref, v_ref)
    for r4 in range(4):
        for kb in range(quarter // BAND):
            for i, src in enumerate(srcs):
                tmp[i, rows(r4 * quarter + kb * BAND), :] = src[pl.ds(r4 + 4 * BAND * kb, BAND, stride=4), :]
    for blk in range(n_blk):
        fill(c4, blk, [tmp[i, rows(blk * BAND), :] for i in range(3)])
    for cls in range(n_blk):
        r4, c = cls % 4, cls // 4
        fill(c16, cls, [tmp[i, pl.ds(r4 * quarter + c, BAND, stride=4), :] for i in range(3)])

    qi = lax.broadcasted_iota(jnp.int32, (2 * BAND, BAND), 0) & (BAND - 1)
    kj = lax.broadcasted_iota(jnp.int32, (2 * BAND, BAND), 1)

    def masks(pos):
        neg = jnp.float32(-jnp.inf)
        prev = jnp.where(pos(kj) >= pos(qi), 0.0, neg)
        cur = jnp.where(pos(kj) <= pos(qi), 0.0, neg)
        return {BAND: cur, 2 * BAND: jnp.concatenate([prev, cur], axis=1)}

    masks1 = masks(lambda i: i)
    sub_shift = SUB.bit_length() - 1
    masks4 = masks(lambda i: 4 * (i & (SUB - 1)) + (i >> sub_shift))

    def block(src, q_rows, k_rows, mask):
        gather = lambda idx, rws: jnp.concatenate([src[idx, r, :] for r in rws], axis=0)
        qs = jnp.concatenate([gather(Q_LO, q_rows), gather(Q_HI, q_rows)], axis=0)
        n_k = sum(r.stop - r.start for r in k_rows)
        s = _dot_nt(qs, gather(K_ALL, k_rows)) + mask[n_k]
        m = jnp.max(s, axis=-1, keepdims=True)
        p = jnp.exp2(s - m).astype(BF16)
        p = jnp.concatenate([p[:BAND], p[BAND:]], axis=1)
        reps = n_k // BAND
        rhs = jnp.concatenate(
            [jnp.concatenate([gather(V_LO, k_rows), jnp.concatenate([ones_lo] * reps, axis=0)], axis=1),
             jnp.concatenate([gather(V_HI, k_rows), jnp.concatenate([ones_hi] * reps, axis=0)], axis=1)],
            axis=0)
        r = _dot(p, rhs)
        return r[:, :LANES], r[:, LANES:], jnp.where(lo, m[:BAND], m[BAND:])

    def put(st, dst_rows, state):
        for i in range(3):
            st[i, dst_rows, :] = state[i]

    get = lambda st, r: [st[i, r, :] for i in range(3)]
    part = lambda state, j: [a[j * SUB:(j + 1) * SUB] for a in state]

    for cls in range(n_blk):
        put(st16, rows(cls * BAND), block(c16, [rows(cls * BAND)], [rows(cls * BAND)], masks1))

    for r4 in range(4):
        pieces = lambda kb: [rows((r4 + 4 * c) * BAND + kb * SUB, SUB) for c in range(4)]
        for kb in range(BAND // SUB):
            k_rows = pieces(kb) if kb == 0 else pieces(kb - 1) + pieces(kb)
            state = block(c16, pieces(kb), k_rows, masks4)
            for c, piece in enumerate(pieces(kb)):
                put(tmp, pl.ds(r4 * quarter + kb * BAND + c, SUB, stride=4),
                    _merge(part(state, c), get(st16, piece)))

    for blk in range(n_blk):
        pieces = lambda b: [rows(r4 * quarter + b * SUB, SUB) for r4 in range(4)]
        k_rows = pieces(blk) if blk == 0 else pieces(blk - 1) + pieces(blk)
        state = block(c4, pieces(blk), k_rows, masks4)
        for r4, piece in enumerate(pieces(blk)):
            acc, den, _ = _merge(part(state, r4), get(tmp, piece))
            y_ref[pl.ds(blk * BAND + r4, SUB, stride=4), :] = acc / den


def _attn(q, k, v, batch):
    t = q.shape[0]
    s_len = t // batch
    assert s_len == 16 * BAND and BAND == 4 * SUB
    slab = pl.BlockSpec((s_len, LANES), lambda b, j: (b, j))
    operands = pltpu.VMEM((5, s_len, LANES), BF16)
    state = pltpu.VMEM((3, s_len, LANES), F32)
    return pl.pallas_call(
        _attn_kernel,
        grid=(batch, D_ATT // LANES),
        in_specs=[slab, slab, slab],
        out_specs=slab,
        out_shape=jax.ShapeDtypeStruct((t, D_ATT), F32),
        scratch_shapes=[operands, operands, state, state],
        compiler_params=_params(2),
        name="dilated_attn",
    )(q, k, v)


def _cross_kernel(x_ref, ya_ref, yb_ref, nb_ref, wmix_hbm, g_ref, wq_hbm, m_ref, mg_ref, wk_hbm, wv_hbm,
                  wo_hbm, out_ref, wmix_ref, wq_ref, wo_ref, wk_ref, wv_ref, k_scr, v_scr, stage, sem):
    first_tile = pl.program_id(1) == 0

    @pl.when((pl.program_id(0) == 0) & first_tile)
    def _():
        for w_hbm, w_ref in ((wk_hbm, wk_ref), (wv_hbm, wv_ref), (wmix_hbm, wmix_ref),
                             (wq_hbm, wq_ref), (wo_hbm, wo_ref)):
            _stream_weight(w_hbm, w_ref, stage, sem)

    @pl.when(first_tile)
    def _():
        mn = _rms(m_ref[...], mg_ref[...]).astype(BF16)
        k_scr[...] = (_dot(mn, wk_ref[...]) * (X_HEAD_DIM ** -0.5)).astype(BF16)
        v_scr[...] = _dot(mn, wv_ref[...]).astype(BF16)

    yb = _rms(yb_ref[...], nb_ref[...]).astype(BF16)
    x = (x_ref[...] + _dot(ya_ref[...], wmix_ref[0:D_SGU, :])
         + _dot(yb, wmix_ref[D_SGU:D_SGU + D_ATT, :]))
    h = _rms(x, g_ref[...]).astype(BF16)
    q = _dot(h, wq_ref[...]).astype(BF16)
    heads = []
    for hh in range(X_HEADS):
        cols = slice(hh * X_HEAD_DIM, (hh + 1) * X_HEAD_DIM)
        s = _dot_nt(q[:, cols], k_scr[:, cols])
        p = jnp.exp(s - jnp.max(s, axis=-1, keepdims=True))
        l = jnp.sum(p, axis=-1, keepdims=True)
        heads.append((_dot(p.astype(BF16), v_scr[:, cols]) / l).astype(BF16))
    o = jnp.concatenate(heads, axis=1)
    out_ref[...] = x + _dot(o, wo_ref[...])


def _cross(x, ya, yb, norm_b, w_mix, g, wq, mem, mem_g, wk, wv, wo):
    t = x.shape[0]
    batch = mem.shape[0]
    per_batch = t // batch // CROSS_TILE
    full = pl.BlockSpec((CROSS_TILE, D_MODEL), lambda b, i: (b * per_batch + i, 0))
    half = pl.BlockSpec((CROSS_TILE, D_ATT), lambda b, i: (b * per_batch + i, 0))
    mem_rows = pl.BlockSpec((None, N_MEM, D_MODEL), lambda b, i: (b, 0, 0))
    hbm = pl.BlockSpec(memory_space=pl.ANY)
    weight = pltpu.VMEM((D_MODEL, D_MODEL), BF16)
    mem_proj = pltpu.VMEM((N_MEM, D_MODEL), BF16)
    return pl.pallas_call(
        _cross_kernel,
        grid=(batch, per_batch),
        in_specs=[full, half, half, _const_spec((1, D_ATT)), hbm, _const_spec((1, D_MODEL)), hbm,
                  mem_rows, _const_spec((1, D_MODEL)), hbm, hbm, hbm],
        out_specs=full,
        out_shape=jax.ShapeDtypeStruct((t, D_MODEL), F32),
        scratch_shapes=[weight] * 5 + [mem_proj, mem_proj, pltpu.VMEM((2, W_STAGE_ROWS, D_MODEL), F32),
                                       pltpu.SemaphoreType.DMA((2,))],
        compiler_params=_params(2),
        name="mix_out_cross",
    )(x, ya, yb, norm_b, w_mix, g, wq, mem, mem_g, wk, wv, wo)


def _rope_lane_freq():
    half = ROPE_DIM // 2
    inv_freq = ROPE_THETA ** (-2.0 * jnp.arange(half, dtype=F32) / ROPE_DIM)
    e = np.arange(LANES) % HEAD_DIM
    pick = (e[None, :] < ROPE_DIM) & (e[None, :] % half == np.arange(half)[:, None])
    return jnp.sum(jnp.where(pick, inv_freq[:, None], 0.0), axis=0, keepdims=True)


def _group_mean_matrix():
    g = np.arange(D_SGU) // SGU_GROUP_DIM
    return jnp.asarray(np.where(g[:, None] == g[None, :], 1.0 / SGU_GROUP_DIM, 0.0), dtype=BF16)


def kernel(x, mem, positions, ffn1_norm, ffn1_w_gate, ffn1_w_up, ffn1_w_down, mix_norm, w_in, sgu_ln_g, sgu_ln_b, sgu_w_s, sgu_b_s, out_norm_a, out_norm_b, w_out, cross_norm, mem_norm, cross_wq, cross_wk, cross_wv, cross_wo, ffn2_norm, ffn2_w_gate, ffn2_w_up, ffn2_w_down, final_norm):
    b, s, d = x.shape
    t = b * s
    depth = ffn1_norm.shape[0]
    row = lambda p: p.reshape(1, -1).astype(F32)
    xt = x.reshape(t, d)
    pos = positions.reshape(t // IN_TILE, IN_TILE // LANES, LANES)
    freq = _rope_lane_freq()
    gmat = _group_mean_matrix()
    fin = row(final_norm)
    for l in range(depth):
        xt = _ffn(xt, row(ffn1_norm[l]), ffn1_w_gate[l], ffn1_w_up[l], ffn1_w_down[l], fin,
                  final_norm=False)

        ya, q, k, v = _in_proj(xt, pos, row(mix_norm[l]), w_in[l], row(sgu_ln_g[l]),
                               row(sgu_ln_b[l]), sgu_w_s[l], sgu_b_s[l].T, gmat, freq,
                               row(out_norm_a[l]))
        yb = _attn(q, k, v, b)
        xt = _cross(xt, ya, yb, row(out_norm_b[l]), w_out[l], row(cross_norm[l]), cross_wq[l],
                    mem, row(mem_norm[l]), cross_wk[l], cross_wv[l], cross_wo[l])

        xt = _ffn(xt, row(ffn2_norm[l]), ffn2_w_gate[l], ffn2_w_up[l], ffn2_w_down[l], fin,
                  final_norm=(l == depth - 1))
    return xt.reshape(b, s, d)
```
